```python
import jax, jax.numpy as jnp
from jax import lax
import numpy as np

D_MODEL = 2048
BATCH = 8
SEQ = 2048
DEPTH = 2
DEC_BATCH = 128
DEC_SEQ = 4
PAST_LEN = 2048
PAGE_SIZE = 128

CONV_W = D_MODEL // 4
CONV_GROUPS = 4
CONV_K = 31
RET_HD = 128
RET_HEADS = (3 * D_MODEL // 8) // RET_HD
RET_W = RET_HEADS * RET_HD
SB_HD = 128
SB_HEADS = (3 * D_MODEL // 8) // SB_HD
SB_W = SB_HEADS * SB_HD
MIX_W = CONV_W + RET_W + SB_W
IN_COLS = 2 * CONV_W + 4 * RET_W + 3 * SB_W
D_FF = 5632
FFN_K = 3
RET_CHUNK = 128
SB_BLOCK = 128
SB_BIAS_INIT = -8.0
ROPE_BASE = 10000.0
EPS = 1e-6

kernel_name = 'hybrid_conv_retention_stickbreak_step'


def rmsnorm(x, g):
    xf = x.astype(jnp.float32)
    y = xf * lax.rsqrt(jnp.mean(xf * xf, axis=-1, keepdims=True) + EPS)
    return (y * g.astype(jnp.float32)).astype(x.dtype)


def group_norm(x, n_groups, g, b=None):
    shp = x.shape
    xf = x.astype(jnp.float32).reshape(shp[:-1] + (n_groups, shp[-1] // n_groups))
    mu = jnp.mean(xf, axis=-1, keepdims=True)
    var = jnp.mean(jnp.square(xf - mu), axis=-1, keepdims=True)
    y = ((xf - mu) * lax.rsqrt(var + EPS)).reshape(shp) * g.astype(jnp.float32)
    if b is not None:
        y = y + b.astype(jnp.float32)
    return y.astype(x.dtype)


def depthwise_conv(x_full, w, b):
    c = x_full.shape[-1]
    y = lax.conv_general_dilated(x_full, w[:, None, :], window_strides=(1,), padding='VALID',
                                 dimension_numbers=('NWC', 'WIO', 'NWC'), feature_group_count=c)
    return y + b


def rope(x, pos):
    half = x.shape[-1] // 2
    inv = ROPE_BASE ** (-jnp.arange(half, dtype=jnp.float32) / half)
    ang = pos.astype(jnp.float32)[:, None] * inv[None, :]
    cos = jnp.cos(ang)[None, :, None, :]
    sin = jnp.sin(ang)[None, :, None, :]
    x1 = x[..., :half].astype(jnp.float32)
    x2 = x[..., half:].astype(jnp.float32)
    return jnp.concatenate([x1 * cos - x2 * sin, x1 * sin + x2 * cos], axis=-1).astype(x.dtype)


def retention_chunk(S, q, k, v, log_gamma):
    L = q.shape[1]
    idx = jnp.arange(L, dtype=jnp.float32)
    diff = idx[:, None] - idx[None, :]
    decay = jnp.where(diff >= 0, jnp.exp(jnp.maximum(diff, 0.0)[None] * log_gamma[:, None, None]), 0.0)
    qf, kf, vf = q.astype(jnp.float32), k.astype(jnp.float32), v.astype(jnp.float32)
    scores = jnp.einsum('nlhd,nmhd->nhlm', qf, kf) * decay[None]
    inner = jnp.einsum('nhlm,nmhe->nlhe', scores, vf)
    xi = jnp.exp((idx[:, None] + 1.0) * log_gamma[None, :])
    cross = jnp.einsum('nlhd,nhde->nlhe', qf, S) * xi[None, :, :, None]
    zeta = jnp.exp((L - 1.0 - idx)[:, None] * log_gamma[None, :])
    S_new = (jnp.exp(L * log_gamma)[None, :, None, None] * S
             + jnp.einsum('nlhd,nlhe->nhde', kf * zeta[None, :, :, None], vf))
    return S_new, inner + cross


def stick_breaking(q, k, v, q_pos, k_pos, bias):
    z = (jnp.einsum('nqhd,nkhd->nhqk', q.astype(jnp.float32), k.astype(jnp.float32)) * (SB_HD ** -0.5)
         + bias.astype(jnp.float32)[None, :, None, None])
    mask = (k_pos[None, :] < q_pos[:, None])[None, None]
    log_keep = jnp.where(mask, jax.nn.log_sigmoid(-z), 0.0)
    after = lax.cumsum(log_keep, axis=3, reverse=True) - log_keep
    w = jnp.where(mask, jnp.exp(jax.nn.log_sigmoid(z) + after), 0.0)
    return jnp.einsum('nhqk,nkhd->nqhd', w, v.astype(jnp.float32))


def sb_prompt(q, k, v, bias):
    N, T, H, d = q.shape
    nb = T // SB_BLOCK
    qb = q.reshape(N, nb, SB_BLOCK, H, d).swapaxes(0, 1)
    k_pos = jnp.arange(T)

    def block(args):
        qi, i = args
        q_pos = i * SB_BLOCK + jnp.arange(SB_BLOCK)
        return stick_breaking(qi, k, v, q_pos, k_pos, bias)

    ob = lax.map(block, (qb, jnp.arange(nb)))
    return ob.swapaxes(0, 1).reshape(N, T, H, d)


def layer(x, pos, conv_prev, ret_S, ffn_prev, past_k, past_v,
          g_pre_mix, w_in, w_dw, b_dw, gn_conv_g, gn_conv_b, gn_ret_g, sb_bias, w_out, g_post_mix,
          g_pre_ffn, w_up, w_fconv, b_fconv, w_down, g_post_ffn):
    N, T, _ = x.shape
    h = rmsnorm(x, g_pre_mix)
    proj = jnp.einsum('ntd,dc->ntc', h, w_in)
    sizes = (CONV_W, CONV_W, RET_W, RET_W, RET_W, RET_W, SB_W, SB_W, SB_W)
    offs = np.cumsum(sizes)[:-1].tolist()
    a_val, a_gate, rq, rk, rv, rg, sq, sk, sv = jnp.split(proj, offs, axis=-1)

    a = a_val * jax.nn.sigmoid(a_gate)
    a_full = jnp.concatenate([conv_prev.astype(a.dtype), a], axis=1)
    a_c = depthwise_conv(a_full, w_dw, b_dw)
    a_out = jax.nn.silu(group_norm(a_c, CONV_GROUPS, gn_conv_g, gn_conv_b))
    new_conv = a_full[:, -(CONV_K - 1):]

    rq = rope(rq.reshape(N, T, RET_HEADS, RET_HD), pos)
    rk = rope(rk.reshape(N, T, RET_HEADS, RET_HD), pos) * (RET_HD ** -0.5)
    rv = rv.reshape(N, T, RET_HEADS, RET_HD)
    L = min(RET_CHUNK, T)
    nC = T // L
    to_chunks = lambda t: t.reshape(N, nC, L, RET_HEADS, RET_HD).swapaxes(0, 1)
    log_gamma = jnp.log1p(-jnp.exp2(-5.0 - jnp.arange(RET_HEADS, dtype=jnp.float32)))

    def step(S, qkv):
        return retention_chunk(S, qkv[0], qkv[1], qkv[2], log_gamma)

    S_new, r = lax.scan(step, ret_S.astype(jnp.float32), (to_chunks(rq), to_chunks(rk), to_chunks(rv)))
    r = r.swapaxes(0, 1).reshape(N, T, RET_W)
    b_out = (group_norm(r, RET_HEADS, gn_ret_g) * jax.nn.silu(rg)).astype(x.dtype)

    sq = sq.reshape(N, T, SB_HEADS, SB_HD)
    sk = sk.reshape(N, T, SB_HEADS, SB_HD)
    sv = sv.reshape(N, T, SB_HEADS, SB_HD)
    if past_k is None:
        c = sb_prompt(sq, sk, sv, sb_bias)
    else:
        k_all = jnp.concatenate([past_k.astype(sk.dtype), sk], axis=1)
        v_all = jnp.concatenate([past_v.astype(sv.dtype), sv], axis=1)
        c = stick_breaking(sq, k_all, v_all, pos, jnp.arange(k_all.shape[1]), sb_bias)
    c_out = c.reshape(N, T, SB_W).astype(x.dtype)

    mix = jnp.einsum('ntc,cd->ntd', jnp.concatenate([a_out, b_out, c_out], axis=-1), w_out)
    x = x + rmsnorm(mix, g_post_mix)

    h = rmsnorm(x, g_pre_ffn)
    u, g = jnp.split(jnp.einsum('ntd,df->ntf', h, w_up), 2, axis=-1)
    g_full = jnp.concatenate([ffn_prev.astype(g.dtype), g], axis=1)
    g_c = depthwise_conv(g_full, w_fconv, b_fconv)
    f = jnp.einsum('ntf,fd->ntd', jax.nn.silu(g_c) * u, w_down)
    x = x + rmsnorm(f, g_post_ffn)
    new_ffn = g_full[:, -(FFN_K - 1):]
    return x, new_conv, S_new.astype(x.dtype), sk, sv, new_ffn


def setup_inputs(seed: int = 0) -> dict:
    key = jax.random.key(seed)
    ks = jax.random.split(key, 32)
    n_pages = PAST_LEN // PAGE_SIZE
    n_pool = (DEC_BATCH * n_pages * 5) // 4
    nrm = lambda k, shape, scale: jax.random.normal(k, shape, jnp.float32) * scale
    gain = lambda k, shape: 1.0 + 0.02 * jax.random.normal(k, shape, jnp.float32)
    perm = jax.random.permutation(ks[0], n_pool)
    page_table = perm[:DEC_BATCH * n_pages].reshape(DEC_BATCH, n_pages).astype(jnp.int32)
    return {
        'x_prompt': nrm(ks[1], (BATCH, SEQ, D_MODEL), 1.0),
        'x_sample': nrm(ks[2], (DEC_BATCH, DEC_SEQ, D_MODEL), 1.0),
        'cache_k': nrm(ks[3], (DEPTH, n_pool, PAGE_SIZE, SB_HEADS, SB_HD), 1.0),
        'cache_v': nrm(ks[4], (DEPTH, n_pool, PAGE_SIZE, SB_HEADS, SB_HD), 1.0),
        'state_conv': nrm(ks[5], (DEPTH, DEC_BATCH, CONV_K - 1, CONV_W), 0.5),
        'state_ret': nrm(ks[6], (DEPTH, DEC_BATCH, RET_HEADS, RET_HD, RET_HD), 0.5),
        'state_ffn': nrm(ks[7], (DEPTH, DEC_BATCH, FFN_K - 1, D_FF), 1.0),
        'page_table': page_table,
        'g_pre_mix': gain(ks[8], (DEPTH, D_MODEL)),
        'w_in': nrm(ks[9], (DEPTH, D_MODEL, IN_COLS), D_MODEL ** -0.5),
        'w_dw': nrm(ks[10], (DEPTH, CONV_K, CONV_W), CONV_K ** -0.5),
        'b_dw': nrm(ks[11], (DEPTH, CONV_W), 0.02),
        'gn_conv_g': gain(ks[12], (DEPTH, CONV_W)),
        'gn_conv_b': nrm(ks[13], (DEPTH, CONV_W), 0.02),
        'gn_ret_g': gain(ks[14], (DEPTH, RET_W)),
        'sb_bias': SB_BIAS_INIT + nrm(ks[23], (DEPTH, SB_HEADS), 0.1),
        'w_out': nrm(ks[15], (DEPTH, MIX_W, D_MODEL), MIX_W ** -0.5),
        'g_post_mix': gain(ks[16], (DEPTH, D_MODEL)),
        'g_pre_ffn': gain(ks[17], (DEPTH, D_MODEL)),
        'w_up': nrm(ks[18], (DEPTH, D_MODEL, 2 * D_FF), D_MODEL ** -0.5),
        'w_fconv': nrm(ks[19], (DEPTH, FFN_K, D_FF), FFN_K ** -0.5),
        'b_fconv': nrm(ks[20], (DEPTH, D_FF), 0.02),
        'w_down': nrm(ks[21], (DEPTH, D_FF, D_MODEL), D_FF ** -0.5),
        'g_post_ffn': gain(ks[22], (DEPTH, D_MODEL)),
    }


def reference(x_prompt, x_sample, cache_k, cache_v, state_conv, state_ret, state_ffn, page_table,
              g_pre_mix, w_in, w_dw, b_dw, gn_conv_g, gn_conv_b, gn_ret_g, sb_bias, w_out, g_post_mix,
              g_pre_ffn, w_up, w_fconv, b_fconv, w_down, g_post_ffn):
    Bp, Tp, _ = x_prompt.shape
    Bs, Ts, _ = x_sample.shape
    pos_p = jnp.arange(Tp)
    pos_s = PAST_LEN + jnp.arange(Ts)
    dt = x_prompt.dtype
    xp, xs = x_prompt, x_sample
    pc, pr, pk, pv, pf = [], [], [], [], []
    sc, sr, skl, svl, sf = [], [], [], [], []
    for l in range(DEPTH):
        w = (g_pre_mix[l], w_in[l], w_dw[l], b_dw[l], gn_conv_g[l], gn_conv_b[l], gn_ret_g[l], sb_bias[l],
             w_out[l], g_post_mix[l], g_pre_ffn[l], w_up[l], w_fconv[l], b_fconv[l], w_down[l], g_post_ffn[l])
        xp, c0, r0, k0, v0, f0 = layer(
            xp, pos_p,
            jnp.zeros((Bp, CONV_K - 1, CONV_W), dt),
            jnp.zeros((Bp, RET_HEADS, RET_HD, RET_HD), jnp.float32),
            jnp.zeros((Bp, FFN_K - 1, D_FF), dt),
            None, None, *w)
        pc.append(c0); pr.append(r0); pk.append(k0); pv.append(v0); pf.append(f0)
        past_k = cache_k[l][page_table].reshape(Bs, -1, SB_HEADS, SB_HD)
        past_v = cache_v[l][page_table].reshape(Bs, -1, SB_HEADS, SB_HD)
        xs, c1, r1, k1, v1, f1 = layer(
            xs, pos_s, state_conv[l], state_ret[l], state_ffn[l], past_k, past_v, *w)
        sc.append(c1); sr.append(r1); skl.append(k1); svl.append(v1); sf.append(f1)
    p_conv = jnp.stack(pc)
    p_ret = jnp.stack(pr)
    p_k = jnp.stack(pk).reshape(DEPTH, Bp, Tp // PAGE_SIZE, PAGE_SIZE, SB_HEADS, SB_HD)
    p_v = jnp.stack(pv).reshape(DEPTH, Bp, Tp // PAGE_SIZE, PAGE_SIZE, SB_HEADS, SB_HD)
    p_ffn = jnp.stack(pf)
    s_conv = jnp.stack(sc)
    s_ret = jnp.stack(sr)
    s_k = jnp.stack(skl)
    s_v = jnp.stack(svl)
    s_ffn = jnp.stack(sf)
    return (xp, xs, p_conv, p_ret, p_k, p_v, p_ffn, s_conv, s_ret, s_k, s_v, s_ffn)
```

```python
import functools

import numpy as np
import jax
import jax.numpy as jnp
from jax import lax
from jax.experimental import pallas as pl
from jax.experimental.pallas import tpu as pltpu

F32 = jnp.float32
BF16 = jnp.bfloat16

EPS = 1e-6
HEAD_DIM = 128
CONV_K = 31
CONV_PAD = 32
FFN_K = 3
ROPE_BASE = 10000.0
RET_CHUNK = 128
V7X_VMEM_BYTES = 64 * 1024 * 1024
VMEM_LIMIT = V7X_VMEM_BYTES - 8 * 1024 * 1024


def _cparams(sem):
    return pltpu.CompilerParams(dimension_semantics=sem, vmem_limit_bytes=VMEM_LIMIT)


def _rms(x, g):
    ms = jnp.mean(x * x, axis=-1, keepdims=True)
    return x * lax.rsqrt(ms + EPS) * g


def _group_norm(y):
    mu = jnp.mean(y, axis=-1, keepdims=True)
    d = y - mu
    var = jnp.mean(d * d, axis=-1, keepdims=True)
    return d * lax.rsqrt(var + EPS)


def _silu(x):
    return x * jax.nn.sigmoid(x)


def _softplus(z):
    return jnp.maximum(z, 0.0) + jnp.log1p(jnp.exp(-jnp.abs(z)))


def _dot(a, b):
    return jnp.dot(a, b, preferred_element_type=F32)


def _dot_nt(a, b):
    return lax.dot_general(a, b, (((1,), (1,)), ((), ())), preferred_element_type=F32)


def _dot_tn(a, b):
    return lax.dot_general(a, b, (((0,), (0,)), ((), ())), preferred_element_type=F32)


def _in_proj_kernel(x_ref, g_ref, w_ref, o_ref, h_ref):
    @pl.when(pl.program_id(1) == 0)
    def _():
        h_ref[...] = _rms(x_ref[...], g_ref[...]).astype(BF16)

    o_ref[...] = _dot(h_ref[...], w_ref[...])


def _in_proj(x, g, w, bm, bn):
    rows, d = x.shape
    n = w.shape[1]
    return pl.pallas_call(
        _in_proj_kernel,
        grid=(rows // bm, n // bn),
        in_specs=[
            pl.BlockSpec((bm, d), lambda i, j: (i, 0)),
            pl.BlockSpec((1, d), lambda i, j: (0, 0)),
            pl.BlockSpec((d, bn), lambda i, j: (0, j)),
        ],
        out_specs=pl.BlockSpec((bm, bn), lambda i, j: (i, j)),
        out_shape=jax.ShapeDtypeStruct((rows, n), F32),
        scratch_shapes=[pltpu.VMEM((bm, d), BF16)],
        compiler_params=_cparams(("parallel", "arbitrary")),
        name="in_proj",
    )(x, g.reshape(1, d), w)


def _conv_prompt_kernel(val_ref, gate_ref, w_ref, b_ref, gg_ref, gb_ref, o_ref, st_ref, abuf):
    tb = val_ref.shape[0]
    sub = min(tb, 128)

    @pl.when(pl.program_id(2) == 0)
    def _():
        abuf[0:CONV_PAD, :] = jnp.zeros((CONV_PAD, HEAD_DIM), F32)

    abuf[CONV_PAD:CONV_PAD + tb, :] = val_ref[...] * jax.nn.sigmoid(gate_ref[...])
    lead = CONV_PAD - (CONV_K - 1)
    for r0 in range(0, tb, sub):
        y = jnp.broadcast_to(b_ref[...], (sub, HEAD_DIM))
        for k in range(CONV_K):
            y = y + w_ref[k:k + 1, :] * abuf[r0 + k + lead:r0 + k + lead + sub, :]
        yn = _group_norm(y) * gg_ref[...] + gb_ref[...]
        o_ref[r0:r0 + sub, :] = _silu(yn).astype(o_ref.dtype)
    tail = abuf[tb:tb + CONV_PAD, :]
    st_ref[0] = tail
    abuf[0:CONV_PAD, :] = tail


def _conv_prompt(proj, n_seq, seq, w_dw, b_dw, gn_g, gn_b, tb):
    conv_w = w_dw.shape[1]
    groups = conv_w // HEAD_DIM
    nt = seq // tb
    return pl.pallas_call(
        _conv_prompt_kernel,
        grid=(n_seq, groups, nt),
        in_specs=[
            pl.BlockSpec((tb, HEAD_DIM), lambda n, g, t: (n * nt + t, g)),
            pl.BlockSpec((tb, HEAD_DIM), lambda n, g, t: (n * nt + t, groups + g)),
            pl.BlockSpec((CONV_K, HEAD_DIM), lambda n, g, t: (0, g)),
            pl.BlockSpec((1, HEAD_DIM), lambda n, g, t: (0, g)),
            pl.BlockSpec((1, HEAD_DIM), lambda n, g, t: (0, g)),
            pl.BlockSpec((1, HEAD_DIM), lambda n, g, t: (0, g)),
        ],
        out_specs=[
            pl.BlockSpec((tb, HEAD_DIM), lambda n, g, t: (n * nt + t, g)),
            pl.BlockSpec((1, CONV_PAD, HEAD_DIM), lambda n, g, t: (n, 0, g)),
        ],
        out_shape=[
            jax.ShapeDtypeStruct((n_seq * seq, conv_w), BF16),
            jax.ShapeDtypeStruct((n_seq, CONV_PAD, conv_w), F32),
        ],
        scratch_shapes=[pltpu.VMEM((CONV_PAD + tb, HEAD_DIM), F32)],
        compiler_params=_cparams(("parallel", "parallel", "arbitrary")),
        name="conv_prompt",
    )(proj, proj, w_dw, b_dw.reshape(1, -1), gn_g.reshape(1, -1), gn_b.reshape(1, -1))


def _conv_sample_kernel(val_ref, gate_ref, st_ref, w_ref, b_ref, gg_ref, gb_ref, o_ref, a_ref, *, nb, ts, n_prev):
    a_ref[...] = val_ref[...] * jax.nn.sigmoid(gate_ref[...])

    def a_full(j):
        if j < n_prev:
            return st_ref[pl.ds(j, nb, stride=n_prev), :]
        return a_ref[pl.ds(j - n_prev, nb, stride=ts), :]

    for t in range(ts):
        y = jnp.broadcast_to(b_ref[...], (nb, HEAD_DIM))
        for k in range(CONV_K):
            y = y + w_ref[k:k + 1, :] * a_full(t + k)
        yn = _group_norm(y) * gg_ref[...] + gb_ref[...]
        o_ref[pl.ds(t, nb, stride=ts), :] = _silu(yn)


def _conv_sample(proj, state2d, nb, ts, w_dw, b_dw, gn_g, gn_b):
    conv_w = w_dw.shape[1]
    groups = conv_w // HEAD_DIM
    n_prev = CONV_K - 1
    rows = nb * ts
    kern = functools.partial(_conv_sample_kernel, nb=nb, ts=ts, n_prev=n_prev)
    return pl.pallas_call(
        kern,
        grid=(groups,),
        in_specs=[
            pl.BlockSpec((rows, HEAD_DIM), lambda g: (0, g)),
            pl.BlockSpec((rows, HEAD_DIM), lambda g: (0, groups + g)),
            pl.BlockSpec((nb * n_prev, HEAD_DIM), lambda g: (0, g)),
            pl.BlockSpec((CONV_K, HEAD_DIM), lambda g: (0, g)),
            pl.BlockSpec((1, HEAD_DIM), lambda g: (0, g)),
            pl.BlockSpec((1, HEAD_DIM), lambda g: (0, g)),
            pl.BlockSpec((1, HEAD_DIM), lambda g: (0, g)),
        ],
        out_specs=[
            pl.BlockSpec((rows, HEAD_DIM), lambda g: (0, g)),
            pl.BlockSpec((rows, HEAD_DIM), lambda g: (0, g)),
        ],
        out_shape=[
            jax.ShapeDtypeStruct((rows, conv_w), F32),
            jax.ShapeDtypeStruct((rows, conv_w), F32),
        ],
        compiler_params=_cparams(("parallel",)),
        name="conv_sample",
    )(proj, proj, state2d, w_dw, b_dw.reshape(1, -1), gn_g.reshape(1, -1), gn_b.reshape(1, -1))


def _rope(x, cos2, sin_signed):
    return x * cos2 + pltpu.roll(x, HEAD_DIM // 2, axis=1) * sin_signed


def _ret_prompt_kernel(q_ref, k_ref, v_ref, g_ref, cos_ref, sin_ref, dec_ref, xi_ref, zeta_ref, gl_ref, gn_ref,
                       o_ref, so_ref, s_ref):
    @pl.when(pl.program_id(2) == 0)
    def _():
        s_ref[...] = jnp.zeros_like(s_ref)

    cos2, sin_s = cos_ref[...], sin_ref[...]
    q = _rope(q_ref[...], cos2, sin_s)
    k = _rope(k_ref[...], cos2, sin_s) * (HEAD_DIM ** -0.5)
    qb, vb = q.astype(BF16), v_ref[...].astype(BF16)
    s = s_ref[...]
    scores = _dot_nt(qb, k.astype(BF16)) * dec_ref[0]
    r = _dot(scores.astype(BF16), vb) + _dot(qb, s.astype(BF16)) * xi_ref[0]
    s_new = gl_ref[0] * s + _dot_tn((k * zeta_ref[0]).astype(BF16), vb)
    s_ref[...] = s_new
    so_ref[0, 0] = s_new
    o_ref[...] = (_group_norm(r) * gn_ref[...] * _silu(g_ref[...])).astype(o_ref.dtype)


def _ret_consts(length, n_heads):
    log_gamma = np.log1p(-np.exp2(-5.0 - np.arange(n_heads, dtype=np.float64)))
    idx = np.arange(RET_CHUNK, dtype=np.float64)
    diff = idx[:, None] - idx[None, :]
    live = (diff >= 0) & (idx[:, None] < length) & (idx[None, :] < length)
    dec = np.where(live[None], np.exp(np.maximum(diff, 0.0)[None] * log_gamma[:, None, None]), 0.0)
    xi = np.exp((idx[None, :] + 1.0) * log_gamma[:, None])
    zeta = np.where(idx[None, :] < length, np.exp((length - 1.0 - idx)[None, :] * log_gamma[:, None]), 0.0)
    gl = np.exp(length * log_gamma)
    bc = lambda a: jnp.asarray(np.broadcast_to(a[:, :, None], (n_heads, RET_CHUNK, HEAD_DIM)), F32)
    return (jnp.asarray(dec, F32), bc(xi), bc(zeta),
            jnp.asarray(np.broadcast_to(gl[:, None, None], (n_heads, 1, HEAD_DIM)), F32))


def _rope_tables(pos):
    half = HEAD_DIM // 2
    inv = ROPE_BASE ** (-jnp.arange(half, dtype=F32) / half)
    ang = pos.astype(F32)[:, None] * inv[None, :]
    cos, sin = jnp.cos(ang), jnp.sin(ang)
    return jnp.concatenate([cos, cos], axis=1), jnp.concatenate([-sin, sin], axis=1)


def _ret_prompt(proj, n_seq, seq, n_heads, col0, cos2, sin_s, gn_g):
    nc = seq // RET_CHUNK
    dec, xi, zeta, gl = _ret_consts(RET_CHUNK, n_heads)
    blk = lambda off: pl.BlockSpec((RET_CHUNK, HEAD_DIM), lambda n, h, c: (n * nc + c, col0 + off * n_heads + h))
    head = lambda rows: pl.BlockSpec((1, rows, HEAD_DIM), lambda n, h, c: (h, 0, 0))
    return pl.pallas_call(
        _ret_prompt_kernel,
        grid=(n_seq, n_heads, nc),
        in_specs=[
            blk(0), blk(1), blk(2), blk(3),
            pl.BlockSpec((RET_CHUNK, HEAD_DIM), lambda n, h, c: (c, 0)),
            pl.BlockSpec((RET_CHUNK, HEAD_DIM), lambda n, h, c: (c, 0)),
            head(RET_CHUNK), head(RET_CHUNK), head(RET_CHUNK), head(1),
            pl.BlockSpec((1, HEAD_DIM), lambda n, h, c: (0, h)),
        ],
        out_specs=[
            pl.BlockSpec((RET_CHUNK, HEAD_DIM), lambda n, h, c: (n * nc + c, h)),
            pl.BlockSpec((1, 1, HEAD_DIM, HEAD_DIM), lambda n, h, c: (n, h, 0, 0)),
        ],
        out_shape=[
            jax.ShapeDtypeStruct((n_seq * seq, n_heads * HEAD_DIM), BF16),
            jax.ShapeDtypeStruct((n_seq, n_heads, HEAD_DIM, HEAD_DIM), F32),
        ],
        scratch_shapes=[pltpu.VMEM((HEAD_DIM, HEAD_DIM), F32)],
        compiler_params=_cparams(("parallel", "parallel", "arbitrary")),
        name="ret_prompt",
    )(proj, proj, proj, proj, cos2, sin_s, dec, xi, zeta, gl, gn_g.reshape(1, -1))


def _ret_sample_kernel(p_ref, s_ref, cos_ref, sin_ref, dec_ref, xi_ref, zeta_ref, gl_ref, gn_ref,
                       o_ref, so_ref, qp, kp, kzp, vp, *, n_heads, col0, ts):
    cos2, sin_s = cos_ref[...], sin_ref[...]
    zeros = jnp.zeros((RET_CHUNK, HEAD_DIM), F32)
    for h in range(n_heads):
        cols = lambda off: slice((col0 + off * n_heads + h) * HEAD_DIM, (col0 + off * n_heads + h + 1) * HEAD_DIM)
        q = _rope(p_ref[0, :, cols(0)], cos2, sin_s)
        k = _rope(p_ref[0, :, cols(1)], cos2, sin_s) * (HEAD_DIM ** -0.5)
        qp[...] = zeros[0:8]
        kp[...] = zeros
        kzp[...] = zeros
        vp[...] = zeros
        qp[0:ts, :] = q
        kp[0:ts, :] = k
        kzp[0:ts, :] = k * zeta_ref[h, 0:ts, :]
        vp[0:ts, :] = p_ref[0, :, cols(2)]
        qb, vb = qp[...].astype(BF16), vp[...].astype(BF16)
        s = s_ref[0, h]
        scores = _dot_nt(qb, kp[...].astype(BF16)) * dec_ref[h, 0:8, :]
        r = _dot(scores.astype(BF16), vb) + _dot(qb, s.astype(BF16)) * xi_ref[h, 0:8, :]
        so_ref[0, h] = gl_ref[h] * s + _dot_tn(kzp[...].astype(BF16), vb)
        gate = p_ref[0, :, cols(3)]
        gn = gn_ref[:, h * HEAD_DIM:(h + 1) * HEAD_DIM]
        o_ref[0, :, h * HEAD_DIM:(h + 1) * HEAD_DIM] = _group_norm(r[0:ts]) * gn * _silu(gate)


def _ret_sample(proj3, state, n_heads, col0, cos2, sin_s, gn_g):
    nb, ts, n_cols = proj3.shape
    dec, xi, zeta, gl = _ret_consts(ts, n_heads)
    kern = functools.partial(_ret_sample_kernel, n_heads=n_heads, col0=col0, ts=ts)
    full = lambda a: pl.BlockSpec(a.shape, lambda b: (0,) * a.ndim)
    return pl.pallas_call(
        kern,
        grid=(nb,),
        in_specs=[
            pl.BlockSpec((1, ts, n_cols), lambda b: (b, 0, 0)),
            pl.BlockSpec((1, n_heads, HEAD_DIM, HEAD_DIM), lambda b: (b, 0, 0, 0)),
            full(cos2), full(sin_s), full(dec), full(xi), full(zeta), full(gl),
            pl.BlockSpec((1, n_heads * HEAD_DIM), lambda b: (0, 0)),
        ],
        out_specs=[
            pl.BlockSpec((1, ts, n_heads * HEAD_DIM), lambda b: (b, 0, 0)),
            pl.BlockSpec((1, n_heads, HEAD_DIM, HEAD_DIM), lambda b: (b, 0, 0, 0)),
        ],
        out_shape=[
            jax.ShapeDtypeStruct((nb, ts, n_heads * HEAD_DIM), F32),
            jax.ShapeDtypeStruct(state.shape, F32),
        ],
        scratch_shapes=[
            pltpu.VMEM((8, HEAD_DIM), F32),
            pltpu.VMEM((RET_CHUNK, HEAD_DIM), F32),
            pltpu.VMEM((RET_CHUNK, HEAD_DIM), F32),
            pltpu.VMEM((RET_CHUNK, HEAD_DIM), F32),
        ],
        compiler_params=_cparams(("parallel",)),
        name="ret_sample",
    )(proj3, state, cos2, sin_s, dec, xi, zeta, gl, gn_g.reshape(1, -1))


SB_BLOCK = 128


def _suffix_matrix():
    j = np.arange(SB_BLOCK)
    strict = (j[:, None] > j[None, :]).astype(np.float32)
    half = np.concatenate([strict, np.ones((SB_BLOCK, SB_BLOCK), np.float32)], axis=1)
    return jnp.asarray(np.concatenate([half, half], axis=0), BF16)


def _sb_block(z, valid, c, suffix, v_b16):
    sp = _softplus(z)
    lk = -sp if valid is None else jnp.where(valid, -sp, 0.0)
    hi = lk.astype(BF16)
    lo = (lk - hi.astype(F32)).astype(BF16)
    cum = _dot(jnp.concatenate([hi, lo], axis=1), suffix)
    w = jnp.exp(z - sp + cum[:, :SB_BLOCK] + c)
    if valid is not None:
        w = jnp.where(valid, w, 0.0)
    return _dot(w.astype(BF16), v_b16), cum[:, SB_BLOCK:]


def _sb_prompt_kernel(q_ref, k_ref, v_ref, bias_ref, suf_ref, o_ref, kb_ref, vb_ref):
    i = pl.program_id(2)

    @pl.when(i == 0)
    def _():
        kb_ref[...] = k_ref[...].astype(BF16)
        vb_ref[...] = v_ref[...].astype(BF16)

    qb = q_ref[...].astype(BF16)
    bias = bias_ref[0]
    row = lax.broadcasted_iota(jnp.int32, (SB_BLOCK, SB_BLOCK), 0)
    col = lax.broadcasted_iota(jnp.int32, (SB_BLOCK, SB_BLOCK), 1)

    def body(jj, carry):
        acc, c = carry
        j = i - jj
        off = pl.multiple_of(j * SB_BLOCK, SB_BLOCK)
        z = _dot_nt(qb, kb_ref[pl.ds(off, SB_BLOCK), :]) * (HEAD_DIM ** -0.5) + bias
        valid = (col + j * SB_BLOCK) < (row + i * SB_BLOCK)
        pv, tot = _sb_block(z, valid, c, suf_ref[...], vb_ref[pl.ds(off, SB_BLOCK), :])
        return acc + pv, c + tot

    zero = jnp.zeros((SB_BLOCK, HEAD_DIM), F32)
    acc, _ = lax.fori_loop(0, i + 1, body, (zero, zero))
    o_ref[...] = acc.astype(o_ref.dtype)


def _sb_prompt(proj, n_seq, seq, n_heads, col0, bias_rows):
    nq = seq // SB_BLOCK
    suffix = _suffix_matrix()
    return pl.pallas_call(
        _sb_prompt_kernel,
        grid=(n_seq, n_heads, nq),
        in_specs=[
            pl.BlockSpec((SB_BLOCK, HEAD_DIM), lambda n, h, i: (n * nq + i, col0 + h)),
            pl.BlockSpec((seq, HEAD_DIM), lambda n, h, i: (n, col0 + n_heads + h)),
            pl.BlockSpec((seq, HEAD_DIM), lambda n, h, i: (n, col0 + 2 * n_heads + h)),
            pl.BlockSpec((1, 1, HEAD_DIM), lambda n, h, i: (h, 0, 0)),
            pl.BlockSpec(suffix.shape, lambda n, h, i: (0, 0)),
        ],
        out_specs=pl.BlockSpec((SB_BLOCK, HEAD_DIM), lambda n, h, i: (n * nq + i, h)),
        out_shape=jax.ShapeDtypeStruct((n_seq * seq, n_heads * HEAD_DIM), BF16),
        scratch_shapes=[pltpu.VMEM((seq, HEAD_DIM), BF16), pltpu.VMEM((seq, HEAD_DIM), BF16)],
        compiler_params=_cparams(("parallel", "parallel", "arbitrary")),
        name="sb_prompt",
    )(proj, proj, proj, bias_rows, suffix)


def _sb_sample_kernel(pt_ref, p_ref, kc_ref, vc_ref, bias_ref, suf_ref, o_ref,
                      qbd, knew, vnew, acc_ref, c_ref, *, n_heads, col0, ts, m_rows):
    p = pl.program_id(1)
    width = n_heads * HEAD_DIM
    sk0 = (col0 + n_heads) * HEAD_DIM
    sv0 = (col0 + 2 * n_heads) * HEAD_DIM

    def sweep(k_flat, v_flat, valid):
        z = _dot_nt(qbd[...].astype(BF16), k_flat.astype(BF16)) * (HEAD_DIM ** -0.5) + bias_ref[...]
        pv, tot = _sb_block(z, valid, c_ref[...], suf_ref[...], v_flat.astype(BF16))
        acc_ref[...] += pv
        c_ref[...] += tot

    @pl.when(p == 0)
    def _():
        qbd[...] = jnp.zeros_like(qbd)
        knew[...] = jnp.zeros_like(knew)
        vnew[...] = jnp.zeros_like(vnew)
        acc_ref[...] = jnp.zeros_like(acc_ref)
        c_ref[...] = jnp.zeros_like(c_ref)
        for h in range(n_heads):
            lanes = slice(h * HEAD_DIM, (h + 1) * HEAD_DIM)
            qbd[h * ts:(h + 1) * ts, lanes] = p_ref[0, :, (col0 + h) * HEAD_DIM:(col0 + h + 1) * HEAD_DIM]
        knew[0:ts, :] = p_ref[0, :, sk0:sk0 + width]
        vnew[0:ts, :] = p_ref[0, :, sv0:sv0 + width]
        row = lax.broadcasted_iota(jnp.int32, (m_rows, SB_BLOCK), 0)
        col = lax.broadcasted_iota(jnp.int32, (m_rows, SB_BLOCK), 1)
        sweep(knew[...], vnew[...], col < lax.rem(row, ts))

    @pl.when(p > 0)
    def _():
        k_flat = jnp.concatenate([kc_ref[:, h, :] for h in range(n_heads)], axis=1)
        v_flat = jnp.concatenate([vc_ref[:, h, :] for h in range(n_heads)], axis=1)
        sweep(k_flat, v_flat, None)

    @pl.when(p == pl.num_programs(1) - 1)
    def _():
        for h in range(n_heads):
            lanes = slice(h * HEAD_DIM, (h + 1) * HEAD_DIM)
            o_ref[0, :, lanes] = acc_ref[h * ts:(h + 1) * ts, lanes]


def _sb_sample(proj3, cache_k, cache_v, layer, page_table, n_heads, col0, sb_bias):
    nb, ts, n_cols = proj3.shape
    n_pages = page_table.shape[1]
    page = cache_k.shape[2]
    width = n_heads * HEAD_DIM
    m_rows = -(-(n_heads * ts) // 8) * 8
    suffix = _suffix_matrix()
    bias_rows = jnp.zeros((m_rows, 1), F32).at[:n_heads * ts, 0].set(jnp.repeat(sb_bias.astype(F32), ts))
    bias_rows = jnp.broadcast_to(bias_rows, (m_rows, SB_BLOCK))
    kern = functools.partial(_sb_sample_kernel, n_heads=n_heads, col0=col0, ts=ts, m_rows=m_rows)

    def page_map(b, p, pt):
        return (layer, pt[b * n_pages + jnp.minimum(n_pages - p, n_pages - 1)], 0, 0, 0)

    grid_spec = pltpu.PrefetchScalarGridSpec(
        num_scalar_prefetch=1,
        grid=(nb, n_pages + 1),
        in_specs=[
            pl.BlockSpec((1, ts, n_cols), lambda b, p, pt: (b, 0, 0)),
            pl.BlockSpec((None, None, page, n_heads, HEAD_DIM), page_map),
            pl.BlockSpec((None, None, page, n_heads, HEAD_DIM), page_map),
            pl.BlockSpec((m_rows, SB_BLOCK), lambda b, p, pt: (0, 0)),
            pl.BlockSpec(suffix.shape, lambda b, p, pt: (0, 0)),
        ],
        out_specs=pl.BlockSpec((1, ts, width), lambda b, p, pt: (b, 0, 0)),
        scratch_shapes=[
            pltpu.VMEM((m_rows, width), F32),
            pltpu.VMEM((SB_BLOCK, width), F32),
            pltpu.VMEM((SB_BLOCK, width), F32),
            pltpu.VMEM((m_rows, width), F32),
            pltpu.VMEM((m_rows, SB_BLOCK), F32),
        ],
    )
    return pl.pallas_call(
        kern,
        grid_spec=grid_spec,
        out_shape=jax.ShapeDtypeStruct((nb, ts, width), F32),
        compiler_params=_cparams(("parallel", "arbitrary")),
        name="sb_sample",
    )(page_table.reshape(-1), proj3, cache_k, cache_v, bias_rows, suffix)


def _out_proj_kernel(a_ref, b_ref, c_ref, w_ref, x_ref, g_ref, o_ref, *, wa, wb):
    mix = (_dot(a_ref[...].astype(BF16), w_ref[0:wa, :])
           + _dot(b_ref[...].astype(BF16), w_ref[wa:wa + wb, :])
           + _dot(c_ref[...].astype(BF16), w_ref[wa + wb:, :]))
    o_ref[...] = x_ref[...] + _rms(mix, g_ref[...])


def _out_proj(a, b, c, w, x, g, bm):
    rows, d = x.shape
    wa, wb, wc = a.shape[1], b.shape[1], c.shape[1]
    kern = functools.partial(_out_proj_kernel, wa=wa, wb=wb)
    return pl.pallas_call(
        kern,
        grid=(rows // bm,),
        in_specs=[
            pl.BlockSpec((bm, wa), lambda i: (i, 0)),
            pl.BlockSpec((bm, wb), lambda i: (i, 0)),
            pl.BlockSpec((bm, wc), lambda i: (i, 0)),
            pl.BlockSpec(w.shape, lambda i: (0, 0)),
            pl.BlockSpec((bm, d), lambda i: (i, 0)),
            pl.BlockSpec((1, d), lambda i: (0, 0)),
        ],
        out_specs=pl.BlockSpec((bm, d), lambda i: (i, 0)),
        out_shape=jax.ShapeDtypeStruct((rows, d), F32),
        compiler_params=_cparams(("parallel",)),
        name="out_proj",
    )(a, b, c, w, x, g.reshape(1, d))


def _ffn_prompt_kernel(x_ref, gpre_ref, wu_ref, wg_ref, wc_ref, bc_ref, wd_ref, gpost_ref,
                       o_ref, tail_ref, h_ref, acc_ref, carry_ref, *, blocks_per_seq):
    i, j = pl.program_id(0), pl.program_id(1)
    bm = x_ref.shape[0]

    @pl.when(j == 0)
    def _():
        h_ref[...] = _rms(x_ref[...], gpre_ref[...]).astype(BF16)
        acc_ref[...] = jnp.zeros_like(acc_ref)

    @pl.when(lax.rem(i, blocks_per_seq) == 0)
    def _():
        carry_ref[j] = jnp.zeros(carry_ref.shape[1:], F32)

    h = h_ref[...]
    u = _dot(h, wu_ref[...])
    g = _dot(h, wg_ref[...])
    prev = carry_ref[j]
    row = lax.broadcasted_iota(jnp.int32, g.shape, 0)
    g1 = jnp.where(row == 0, prev[7:8, :], pltpu.roll(g, 1, axis=0))
    g2 = jnp.where(row == 0, prev[6:7, :], jnp.where(row == 1, prev[7:8, :], pltpu.roll(g, 2, axis=0)))
    gc = wc_ref[0:1, :] * g2 + wc_ref[1:2, :] * g1 + wc_ref[2:3, :] * g + bc_ref[...]
    acc_ref[...] += _dot((_silu(gc) * u).astype(BF16), wd_ref[...])
    tail = g[bm - 8:bm, :]
    carry_ref[j] = tail
    tail_ref[0] = tail

    @pl.when(j == pl.num_programs(1) - 1)
    def _():
        o_ref[...] = x_ref[...] + _rms(acc_ref[...], gpost_ref[...])


def _ffn_sample_kernel(x_ref, gpre_ref, wu_ref, wg_ref, wc_ref, bc_ref, wd_ref, gpost_ref, prev0_ref, prev1_ref,
                       o_ref, tail0_ref, tail1_ref, h_ref, acc_ref, g_scr, u_scr, act_scr, *, nb, ts):
    j = pl.program_id(0)
    n_prev = FFN_K - 1
    prev_refs = (prev0_ref, prev1_ref)
    tail_refs = (tail0_ref, tail1_ref)
    n_chunks = g_scr.shape[0]

    @pl.when(j == 0)
    def _():
        h_ref[...] = _rms(x_ref[...], gpre_ref[...]).astype(BF16)
        acc_ref[...] = jnp.zeros_like(acc_ref)

    h = h_ref[...]
    u = _dot(h, wu_ref[...])
    g = _dot(h, wg_ref[...])
    for c in range(n_chunks):
        lanes = slice(c * HEAD_DIM, (c + 1) * HEAD_DIM)
        u_scr[c] = u[:, lanes]
        g_scr[c] = g[:, lanes]

    for c in range(n_chunks):
        lanes = slice(c * HEAD_DIM, (c + 1) * HEAD_DIM)

        def g_full(t):
            if t < n_prev:
                return prev_refs[t][:, lanes]
            return g_scr[c, pl.ds(t - n_prev, nb, stride=ts), :]

        for t in range(ts):
            gc = bc_ref[:, lanes]
            for k in range(FFN_K):
                gc = gc + wc_ref[k:k + 1, lanes] * g_full(t + k)
            act_scr[c, pl.ds(t, nb, stride=ts), :] = _silu(gc) * u_scr[c, pl.ds(t, nb, stride=ts), :]
        for t in range(n_prev):
            tail_refs[t][:, lanes] = g_full(ts + t)
        acc_ref[...] += _dot(act_scr[c].astype(BF16), wd_ref[lanes, :])

    @pl.when(j == pl.num_programs(0) - 1)
    def _():
        o_ref[...] = x_ref[...] + _rms(acc_ref[...], gpost_ref[...])


def _ffn_prompt(x, gpre, w_up, w_fconv, b_fconv, w_down, gpost, seq, bm, tf):
    rows, d = x.shape
    d_ff = w_down.shape[0]
    nf = d_ff // tf
    kern = functools.partial(_ffn_prompt_kernel, blocks_per_seq=seq // bm)
    return pl.pallas_call(
        kern,
        grid=(rows // bm, nf),
        in_specs=[
            pl.BlockSpec((bm, d), lambda i, j: (i, 0)),
            pl.BlockSpec((1, d), lambda i, j: (0, 0)),
            pl.BlockSpec((d, tf), lambda i, j: (0, j)),
            pl.BlockSpec((d, tf), lambda i, j: (0, nf + j)),
            pl.BlockSpec((FFN_K, tf), lambda i, j: (0, j)),
            pl.BlockSpec((1, tf), lambda i, j: (0, j)),
            pl.BlockSpec((tf, d), lambda i, j: (j, 0)),
            pl.BlockSpec((1, d), lambda i, j: (0, 0)),
        ],
        out_specs=[
            pl.BlockSpec((bm, d), lambda i, j: (i, 0)),
            pl.BlockSpec((1, 8, tf), lambda i, j: (i, 0, j)),
        ],
        out_shape=[
            jax.ShapeDtypeStruct((rows, d), F32),
            jax.ShapeDtypeStruct((rows // bm, 8, d_ff), F32),
        ],
        scratch_shapes=[
            pltpu.VMEM((bm, d), BF16),
            pltpu.VMEM((bm, d), F32),
            pltpu.VMEM((nf, 8, tf), F32),
        ],
        compiler_params=_cparams(("arbitrary", "arbitrary")),
        name="ffn_prompt",
    )(x, gpre.reshape(1, d), w_up, w_up, w_fconv, b_fconv.reshape(1, -1), w_down, gpost.reshape(1, d))


def _ffn_sample(x, gpre, w_up, w_fconv, b_fconv, w_down, gpost, prev, nb, ts, tf):
    rows, d = x.shape
    d_ff = w_down.shape[0]
    nf = d_ff // tf
    kern = functools.partial(_ffn_sample_kernel, nb=nb, ts=ts)
    chunk = pltpu.VMEM((tf // HEAD_DIM, rows, HEAD_DIM), F32)
    return pl.pallas_call(
        kern,
        grid=(nf,),
        in_specs=[
            pl.BlockSpec((rows, d), lambda j: (0, 0)),
            pl.BlockSpec((1, d), lambda j: (0, 0)),
            pl.BlockSpec((d, tf), lambda j: (0, j)),
            pl.BlockSpec((d, tf), lambda j: (0, nf + j)),
            pl.BlockSpec((FFN_K, tf), lambda j: (0, j)),
            pl.BlockSpec((1, tf), lambda j: (0, j)),
            pl.BlockSpec((tf, d), lambda j: (j, 0)),
            pl.BlockSpec((1, d), lambda j: (0, 0)),
            pl.BlockSpec((nb, tf), lambda j: (0, j)),
            pl.BlockSpec((nb, tf), lambda j: (0, nf + j)),
        ],
        out_specs=[
            pl.BlockSpec((rows, d), lambda j: (0, 0)),
            pl.BlockSpec((nb, tf), lambda j: (0, j)),
            pl.BlockSpec((nb, tf), lambda j: (0, j)),
        ],
        out_shape=[
            jax.ShapeDtypeStruct((rows, d), F32),
            jax.ShapeDtypeStruct((nb, d_ff), F32),
            jax.ShapeDtypeStruct((nb, d_ff), F32),
        ],
        scratch_shapes=[pltpu.VMEM((rows, d), BF16), pltpu.VMEM((rows, d), F32), chunk, chunk, chunk],
        compiler_params=_cparams(("arbitrary",)),
        name="ffn_sample",
    )(x, gpre.reshape(1, d), w_up, w_up, w_fconv, b_fconv.reshape(1, -1), w_down, gpost.reshape(1, d), prev, prev)


def _pick(total, want):
    if total <= want:
        return total
    best = None
    for cand in range(128, want + 1, 128):
        if total % cand == 0:
            best = cand
    assert best is not None, (total, want)
    return best


def kernel(x_prompt, x_sample, cache_k, cache_v, state_conv, state_ret, state_ffn, page_table, g_pre_mix, w_in, w_dw, b_dw, gn_conv_g, gn_conv_b, gn_ret_g, sb_bias, w_out, g_post_mix, g_pre_ffn, w_up, w_fconv, b_fconv, w_down, g_post_ffn):
    bp, tp, d = x_prompt.shape
    bs, ts, _ = x_sample.shape
    depth = w_in.shape[0]
    conv_w = w_dw.shape[2]
    n_ret = gn_ret_g.shape[1] // HEAD_DIM
    n_sb = sb_bias.shape[1]
    d_ff = w_down.shape[1]
    past_len = page_table.shape[1] * cache_k.shape[2]
    assert conv_w % HEAD_DIM == 0 and tp % RET_CHUNK == 0 and tp % SB_BLOCK == 0 and ts < 8

    c_ret = 2 * conv_w // HEAD_DIM
    c_sb = c_ret + 4 * n_ret
    sk0 = (c_sb + n_sb) * HEAD_DIM
    sv0 = (c_sb + 2 * n_sb) * HEAD_DIM
    sb_w = n_sb * HEAD_DIM

    bm_p = _pick(bp * tp, 512)
    bm_seq = _pick(tp, 512)
    bn = _pick(w_in.shape[2], 1280)
    tf = _pick(d_ff, 512)

    cos_p, sin_p = _rope_tables(jnp.arange(tp))
    cos_s, sin_s = _rope_tables(past_len + jnp.arange(ts))

    xp = x_prompt.reshape(bp * tp, d)
    xs = x_sample.reshape(bs * ts, d)
    outs = [[] for _ in range(10)]
    for l in range(depth):
        w_in_b = w_in[l].astype(BF16)
        w_out_b = w_out[l].astype(BF16)
        w_up_b = w_up[l].astype(BF16)
        w_down_b = w_down[l].astype(BF16)
        bias_rows = jnp.broadcast_to(sb_bias[l].astype(F32)[:, None, None], (n_sb, 1, HEAD_DIM))

        proj = _in_proj(xp, g_pre_mix[l], w_in_b, bm_p, bn)
        a_out, conv_tail = _conv_prompt(proj, bp, tp, w_dw[l], b_dw[l], gn_conv_g[l], gn_conv_b[l], bm_seq)
        b_out, ret_state = _ret_prompt(proj, bp, tp, n_ret, c_ret, cos_p, sin_p, gn_ret_g[l])
        c_out = _sb_prompt(proj, bp, tp, n_sb, c_sb, bias_rows)
        xp = _out_proj(a_out, b_out, c_out, w_out_b, xp, g_post_mix[l], _pick(bp * tp, 256))
        xp, ffn_tail = _ffn_prompt(xp, g_pre_ffn[l], w_up_b, w_fconv[l], b_fconv[l], w_down_b, g_post_ffn[l],
                                   tp, bm_seq, tf)
        outs[0].append(conv_tail[:, CONV_PAD - (CONV_K - 1):, :])
        outs[1].append(ret_state)
        outs[2].append(proj[:, sk0:sk0 + sb_w].reshape(bp, tp // cache_k.shape[2], cache_k.shape[2], n_sb, HEAD_DIM))
        outs[3].append(proj[:, sv0:sv0 + sb_w].reshape(bp, tp // cache_k.shape[2], cache_k.shape[2], n_sb, HEAD_DIM))
        outs[4].append(ffn_tail.reshape(bp, tp // bm_seq, 8, d_ff)[:, -1, 8 - (FFN_K - 1):, :])

        proj_s = _in_proj(xs, g_pre_mix[l], w_in_b, bs * ts, bn)
        proj_s3 = proj_s.reshape(bs, ts, -1)
        a_s, a_new = _conv_sample(proj_s, state_conv[l].reshape(bs * (CONV_K - 1), conv_w), bs, ts,
                                  w_dw[l], b_dw[l], gn_conv_g[l], gn_conv_b[l])
        b_s, ret_s = _ret_sample(proj_s3, state_ret[l], n_ret, c_ret, cos_s, sin_s, gn_ret_g[l])
        c_s = _sb_sample(proj_s3, cache_k, cache_v, l, page_table, n_sb, c_sb, sb_bias[l])
        xs = _out_proj(a_s, b_s.reshape(bs * ts, -1), c_s.reshape(bs * ts, -1), w_out_b, xs, g_post_mix[l], bs * ts)
        xs, ffn_t0, ffn_t1 = _ffn_sample(xs, g_pre_ffn[l], w_up_b, w_fconv[l], b_fconv[l], w_down_b, g_post_ffn[l],
                                         state_ffn[l].reshape(bs, (FFN_K - 1) * d_ff), bs, ts, tf)
        conv_full = jnp.concatenate([state_conv[l], a_new.reshape(bs, ts, conv_w)], axis=1)
        outs[5].append(conv_full[:, -(CONV_K - 1):, :])
        outs[6].append(ret_s)
        outs[7].append(proj_s3[:, :, sk0:sk0 + sb_w].reshape(bs, ts, n_sb, HEAD_DIM))
        outs[8].append(proj_s3[:, :, sv0:sv0 + sb_w].reshape(bs, ts, n_sb, HEAD_DIM))
        outs[9].append(jnp.stack([ffn_t0, ffn_t1], axis=1))

    stacked = [jnp.stack(o) for o in outs]
    return (xp.reshape(bp, tp, d), xs.reshape(bs, ts, d), *stacked)
```

```python
import functools

import numpy as np
import jax
import jax.numpy as jnp
from jax import lax
from jax.experimental import pallas as pl
from jax.experimental.pallas import tpu as pltpu

F32 = jnp.float32
BF16 = jnp.bfloat16

EPS = 1e-6
HEAD_DIM = 128
CONV_K = 31
CONV_PAD = 32
FFN_K = 3
ROPE_BASE = 10000.0
RET_CHUNK = 128
V7X_VMEM_BYTES = 64 * 1024 * 1024
VMEM_LIMIT = V7X_VMEM_BYTES - 8 * 1024 * 1024


def _cparams(sem):
    return pltpu.CompilerParams(dimension_semantics=sem, vmem_limit_bytes=VMEM_LIMIT)


def _rms(x, g):
    ms = jnp.mean(x * x, axis=-1, keepdims=True)
    return x * lax.rsqrt(ms + EPS) * g


def _group_norm(y):
    mu = jnp.mean(y, axis=-1, keepdims=True)
    d = y - mu
    var = jnp.mean(d * d, axis=-1, keepdims=True)
    return d * lax.rsqrt(var + EPS)


def _silu(x):
    return x * jax.nn.sigmoid(x)


def _softplus(z):
    return jnp.maximum(z, 0.0) + jnp.log(1.0 + jnp.exp(-jnp.abs(z)))


def _dot(a, b):
    return jnp.dot(a, b, preferred_element_type=F32)


def _dot_nt(a, b):
    return lax.dot_general(a, b, (((1,), (1,)), ((), ())), preferred_element_type=F32)


def _dot_tn(a, b):
    return lax.dot_general(a, b, (((0,), (0,)), ((), ())), preferred_element_type=F32)


def _in_proj_kernel(x_ref, g_ref, w_ref, o_ref, h_ref):
    @pl.when(pl.program_id(1) == 0)
    def _():
        h_ref[...] = _rms(x_ref[...], g_ref[...]).astype(BF16)

    o_ref[...] = _dot(h_ref[...], w_ref[...])


def _in_proj(x, g, w, bm, bn):
    rows, d = x.shape
    n = w.shape[1]
    return pl.pallas_call(
        _in_proj_kernel,
        grid=(rows // bm, n // bn),
        in_specs=[
            pl.BlockSpec((bm, d), lambda i, j: (i, 0)),
            pl.BlockSpec((1, d), lambda i, j: (0, 0)),
            pl.BlockSpec((d, bn), lambda i, j: (0, j)),
        ],
        out_specs=pl.BlockSpec((bm, bn), lambda i, j: (i, j)),
        out_shape=jax.ShapeDtypeStruct((rows, n), F32),
        scratch_shapes=[pltpu.VMEM((bm, d), BF16)],
        compiler_params=_cparams(("parallel", "arbitrary")),
        name="in_proj",
    )(x, g.reshape(1, d), w)


def _conv_prompt_kernel(val_ref, gate_ref, w_ref, b_ref, gg_ref, gb_ref, o_ref, st_ref, abuf):
    tb = val_ref.shape[0]
    sub = min(tb, 128)

    @pl.when(pl.program_id(2) == 0)
    def _():
        abuf[0:CONV_PAD, :] = jnp.zeros((CONV_PAD, HEAD_DIM), F32)

    abuf[CONV_PAD:CONV_PAD + tb, :] = val_ref[...] * jax.nn.sigmoid(gate_ref[...])
    lead = CONV_PAD - (CONV_K - 1)
    for r0 in range(0, tb, sub):
        y = jnp.broadcast_to(b_ref[...], (sub, HEAD_DIM))
        for k in range(CONV_K):
            y = y + w_ref[k:k + 1, :] * abuf[r0 + k + lead:r0 + k + lead + sub, :]
        yn = _group_norm(y) * gg_ref[...] + gb_ref[...]
        o_ref[r0:r0 + sub, :] = _silu(yn).astype(o_ref.dtype)
    tail = abuf[tb:tb + CONV_PAD, :]
    st_ref[0] = tail
    abuf[0:CONV_PAD, :] = tail


def _conv_prompt(proj, col0, n_seq, seq, w_dw, b_dw, gn_g, gn_b, tb):
    conv_w = w_dw.shape[1]
    groups = conv_w // HEAD_DIM
    nt = seq // tb
    return pl.pallas_call(
        _conv_prompt_kernel,
        grid=(n_seq, groups, nt),
        in_specs=[
            pl.BlockSpec((tb, HEAD_DIM), lambda n, g, t: (n * nt + t, col0 + g)),
            pl.BlockSpec((tb, HEAD_DIM), lambda n, g, t: (n * nt + t, col0 + groups + g)),
            pl.BlockSpec((CONV_K, HEAD_DIM), lambda n, g, t: (0, g)),
            pl.BlockSpec((1, HEAD_DIM), lambda n, g, t: (0, g)),
            pl.BlockSpec((1, HEAD_DIM), lambda n, g, t: (0, g)),
            pl.BlockSpec((1, HEAD_DIM), lambda n, g, t: (0, g)),
        ],
        out_specs=[
            pl.BlockSpec((tb, HEAD_DIM), lambda n, g, t: (n * nt + t, g)),
            pl.BlockSpec((1, CONV_PAD, HEAD_DIM), lambda n, g, t: (n, 0, g)),
        ],
        out_shape=[
            jax.ShapeDtypeStruct((n_seq * seq, conv_w), BF16),
            jax.ShapeDtypeStruct((n_seq, CONV_PAD, conv_w), F32),
        ],
        scratch_shapes=[pltpu.VMEM((CONV_PAD + tb, HEAD_DIM), F32)],
        compiler_params=_cparams(("parallel", "parallel", "arbitrary")),
        name="conv_prompt",
    )(proj, proj, w_dw, b_dw.reshape(1, -1), gn_g.reshape(1, -1), gn_b.reshape(1, -1))


def _conv_sample_kernel(val_ref, gate_ref, st_ref, w_ref, b_ref, gg_ref, gb_ref, o_ref, a_ref, *, nb, ts, n_prev):
    a_ref[...] = val_ref[...] * jax.nn.sigmoid(gate_ref[...])

    def a_full(j):
        if j < n_prev:
            return st_ref[pl.ds(j, nb, stride=n_prev), :]
        return a_ref[pl.ds(j - n_prev, nb, stride=ts), :]

    for t in range(ts):
        y = jnp.broadcast_to(b_ref[...], (nb, HEAD_DIM))
        for k in range(CONV_K):
            y = y + w_ref[k:k + 1, :] * a_full(t + k)
        yn = _group_norm(y) * gg_ref[...] + gb_ref[...]
        o_ref[pl.ds(t, nb, stride=ts), :] = _silu(yn)


def _conv_sample(proj, col0, state2d, nb, ts, w_dw, b_dw, gn_g, gn_b):
    conv_w = w_dw.shape[1]
    groups = conv_w // HEAD_DIM
    n_prev = CONV_K - 1
    rows = nb * ts
    kern = functools.partial(_conv_sample_kernel, nb=nb, ts=ts, n_prev=n_prev)
    return pl.pallas_call(
        kern,
        grid=(groups,),
        in_specs=[
            pl.BlockSpec((rows, HEAD_DIM), lambda g: (0, col0 + g)),
            pl.BlockSpec((rows, HEAD_DIM), lambda g: (0, col0 + groups + g)),
            pl.BlockSpec((nb * n_prev, HEAD_DIM), lambda g: (0, g)),
            pl.BlockSpec((CONV_K, HEAD_DIM), lambda g: (0, g)),
            pl.BlockSpec((1, HEAD_DIM), lambda g: (0, g)),
            pl.BlockSpec((1, HEAD_DIM), lambda g: (0, g)),
            pl.BlockSpec((1, HEAD_DIM), lambda g: (0, g)),
        ],
        out_specs=[
            pl.BlockSpec((rows, HEAD_DIM), lambda g: (0, g)),
            pl.BlockSpec((rows, HEAD_DIM), lambda g: (0, g)),
        ],
        out_shape=[
            jax.ShapeDtypeStruct((rows, conv_w), F32),
            jax.ShapeDtypeStruct((rows, conv_w), F32),
        ],
        compiler_params=_cparams(("parallel",)),
        name="conv_sample",
    )(proj, proj, state2d, w_dw, b_dw.reshape(1, -1), gn_g.reshape(1, -1), gn_b.reshape(1, -1))


def _rope(x, cos2, sin_signed):
    return x * cos2 + pltpu.roll(x, HEAD_DIM // 2, axis=1) * sin_signed


def _ret_prompt_kernel(q_ref, k_ref, v_ref, g_ref, cos_ref, sin_ref, dec_ref, xi_ref, zeta_ref, gl_ref, gn_ref,
                       o_ref, so_ref, s_ref):
    @pl.when(pl.program_id(1) == 0)
    def _():
        s_ref[...] = jnp.zeros_like(s_ref)

    cos2, sin_s = cos_ref[...], sin_ref[...]
    heads = range(s_ref.shape[0])
    lanes = [slice(h * HEAD_DIM, (h + 1) * HEAD_DIM) for h in heads]
    qs = [_rope(q_ref[:, ln], cos2, sin_s).astype(BF16) for ln in lanes]
    ks = [_rope(k_ref[:, ln], cos2, sin_s) * (HEAD_DIM ** -0.5) for ln in lanes]
    vs = [v_ref[:, ln].astype(BF16) for ln in lanes]
    scores = [_dot_nt(qs[h], ks[h].astype(BF16)) * dec_ref[h] for h in heads]
    cross = [_dot(qs[h], s_ref[h].astype(BF16)) * xi_ref[h] for h in heads]
    kv = [_dot_tn((ks[h] * zeta_ref[h]).astype(BF16), vs[h]) for h in heads]
    inner = [_dot(scores[h].astype(BF16), vs[h]) for h in heads]
    for h in heads:
        s_new = gl_ref[h] * s_ref[h] + kv[h]
        s_ref[h] = s_new
        so_ref[0, h] = s_new
        r = inner[h] + cross[h]
        o_ref[:, lanes[h]] = (_group_norm(r) * gn_ref[:, lanes[h]] * _silu(g_ref[:, lanes[h]])).astype(o_ref.dtype)


def _ret_consts(length, n_heads):
    log_gamma = np.log1p(-np.exp2(-5.0 - np.arange(n_heads, dtype=np.float64)))
    idx = np.arange(RET_CHUNK, dtype=np.float64)
    diff = idx[:, None] - idx[None, :]
    live = (diff >= 0) & (idx[:, None] < length) & (idx[None, :] < length)
    dec = np.where(live[None], np.exp(np.maximum(diff, 0.0)[None] * log_gamma[:, None, None]), 0.0)
    xi = np.exp((idx[None, :] + 1.0) * log_gamma[:, None])
    zeta = np.where(idx[None, :] < length, np.exp((length - 1.0 - idx)[None, :] * log_gamma[:, None]), 0.0)
    gl = np.exp(length * log_gamma)
    bc = lambda a: jnp.asarray(np.broadcast_to(a[:, :, None], (n_heads, RET_CHUNK, HEAD_DIM)), F32)
    return (jnp.asarray(dec, F32), bc(xi), bc(zeta),
            jnp.asarray(np.broadcast_to(gl[:, None, None], (n_heads, 1, HEAD_DIM)), F32))


def _rope_tables(pos):
    half = HEAD_DIM // 2
    inv = ROPE_BASE ** (-jnp.arange(half, dtype=F32) / half)
    ang = pos.astype(F32)[:, None] * inv[None, :]
    cos, sin = jnp.cos(ang), jnp.sin(ang)
    return jnp.concatenate([cos, cos], axis=1), jnp.concatenate([-sin, sin], axis=1)


def _ret_prompt(proj, n_seq, seq, n_heads, seg0, cos2, sin_s, gn_g):
    nc = seq // RET_CHUNK
    width = n_heads * HEAD_DIM
    dec, xi, zeta, gl = _ret_consts(RET_CHUNK, n_heads)
    blk = lambda off: pl.BlockSpec((RET_CHUNK, width), lambda n, c: (n * nc + c, seg0 + off))
    full = lambda a: pl.BlockSpec(a.shape, lambda n, c: (0,) * a.ndim)
    return pl.pallas_call(
        _ret_prompt_kernel,
        grid=(n_seq, nc),
        in_specs=[
            blk(0), blk(1), blk(2), blk(3),
            pl.BlockSpec((RET_CHUNK, HEAD_DIM), lambda n, c: (c, 0)),
            pl.BlockSpec((RET_CHUNK, HEAD_DIM), lambda n, c: (c, 0)),
            full(dec), full(xi), full(zeta), full(gl),
            pl.BlockSpec((1, width), lambda n, c: (0, 0)),
        ],
        out_specs=[
            pl.BlockSpec((RET_CHUNK, width), lambda n, c: (n * nc + c, 0)),
            pl.BlockSpec((1, n_heads, HEAD_DIM, HEAD_DIM), lambda n, c: (n, 0, 0, 0)),
        ],
        out_shape=[
            jax.ShapeDtypeStruct((n_seq * seq, width), BF16),
            jax.ShapeDtypeStruct((n_seq, n_heads, HEAD_DIM, HEAD_DIM), F32),
        ],
        scratch_shapes=[pltpu.VMEM((n_heads, HEAD_DIM, HEAD_DIM), F32)],
        compiler_params=_cparams(("parallel", "arbitrary")),
        name="ret_prompt",
    )(proj, proj, proj, proj, cos2, sin_s, dec, xi, zeta, gl, gn_g.reshape(1, -1))


def _ret_sample_kernel(p_ref, s_ref, cos_ref, sin_ref, dec_ref, xi_ref, zeta_ref, gl_ref, gn_ref,
                       o_ref, so_ref, qp, kp, kzp, vp, *, n_heads, col0, ts):
    cos2, sin_s = cos_ref[...], sin_ref[...]
    heads = range(n_heads)
    cols = lambda off, h: slice((col0 + off * n_heads + h) * HEAD_DIM, (col0 + off * n_heads + h + 1) * HEAD_DIM)
    qp[...] = jnp.zeros_like(qp)
    kp[...] = jnp.zeros_like(kp)
    kzp[...] = jnp.zeros_like(kzp)
    vp[...] = jnp.zeros_like(vp)
    for h in heads:
        k = _rope(p_ref[0, :, cols(1, h)], cos2, sin_s) * (HEAD_DIM ** -0.5)
        qp[h, 0:ts, :] = _rope(p_ref[0, :, cols(0, h)], cos2, sin_s)
        kp[h, 0:ts, :] = k
        kzp[h, 0:ts, :] = k * zeta_ref[h, 0:ts, :]
        vp[h, 0:ts, :] = p_ref[0, :, cols(2, h)]
    qs = [qp[h].astype(BF16) for h in heads]
    vs = [vp[h].astype(BF16) for h in heads]
    scores = [_dot_nt(qs[h], kp[h].astype(BF16)) * dec_ref[h, 0:8, :] for h in heads]
    cross = [_dot(qs[h], s_ref[0, h].astype(BF16)) * xi_ref[h, 0:8, :] for h in heads]
    kv = [_dot_tn(kzp[h].astype(BF16), vs[h]) for h in heads]
    inner = [_dot(scores[h].astype(BF16), vs[h]) for h in heads]
    for h in heads:
        so_ref[0, h] = gl_ref[h] * s_ref[0, h] + kv[h]
        r = (inner[h] + cross[h])[0:ts]
        gn = gn_ref[:, h * HEAD_DIM:(h + 1) * HEAD_DIM]
        o_ref[0, :, h * HEAD_DIM:(h + 1) * HEAD_DIM] = _group_norm(r) * gn * _silu(p_ref[0, :, cols(3, h)])


def _ret_sample(proj3, state, n_heads, col0, cos2, sin_s, gn_g):
    nb, ts, n_cols = proj3.shape
    dec, xi, zeta, gl = _ret_consts(ts, n_heads)
    kern = functools.partial(_ret_sample_kernel, n_heads=n_heads, col0=col0, ts=ts)
    full = lambda a: pl.BlockSpec(a.shape, lambda b: (0,) * a.ndim)
    return pl.pallas_call(
        kern,
        grid=(nb,),
        in_specs=[
            pl.BlockSpec((1, ts, n_cols), lambda b: (b, 0, 0)),
            pl.BlockSpec((1, n_heads, HEAD_DIM, HEAD_DIM), lambda b: (b, 0, 0, 0)),
            full(cos2), full(sin_s), full(dec), full(xi), full(zeta), full(gl),
            pl.BlockSpec((1, n_heads * HEAD_DIM), lambda b: (0, 0)),
        ],
        out_specs=[
            pl.BlockSpec((1, ts, n_heads * HEAD_DIM), lambda b: (b, 0, 0)),
            pl.BlockSpec((1, n_heads, HEAD_DIM, HEAD_DIM), lambda b: (b, 0, 0, 0)),
        ],
        out_shape=[
            jax.ShapeDtypeStruct((nb, ts, n_heads * HEAD_DIM), F32),
            jax.ShapeDtypeStruct(state.shape, F32),
        ],
        scratch_shapes=[
            pltpu.VMEM((n_heads, 8, HEAD_DIM), F32),
            pltpu.VMEM((n_heads, RET_CHUNK, HEAD_DIM), F32),
            pltpu.VMEM((n_heads, RET_CHUNK, HEAD_DIM), F32),
            pltpu.VMEM((n_heads, RET_CHUNK, HEAD_DIM), F32),
        ],
        compiler_params=_cparams(("parallel",)),
        name="ret_sample",
    )(proj3, state, cos2, sin_s, dec, xi, zeta, gl, gn_g.reshape(1, -1))


SB_BLOCK = 128


def _suffix_matrix():
    j = np.arange(SB_BLOCK)
    strict = (j[:, None] > j[None, :]).astype(np.float32)
    half = np.concatenate([strict, np.ones((SB_BLOCK, SB_BLOCK), np.float32)], axis=1)
    return jnp.asarray(np.concatenate([half, half], axis=0), BF16)


def _suffix_sums(sp, valid, suffix):
    lk = -sp if valid is None else jnp.where(valid, -sp, 0.0)
    hi = lk.astype(BF16)
    lo = (lk - hi.astype(F32)).astype(BF16)
    return _dot(jnp.concatenate([hi, lo], axis=1), suffix)


def _sb_prompt_kernel(q_ref, k_ref, v_ref, bias_ref, suf_ref, o_ref, qb_ref, kb_ref, vb_ref, acc_ref, c_ref,
                      *, n_heads):
    i = pl.program_id(1)

    @pl.when(i == 0)
    def _():
        kb_ref[...] = k_ref[...].astype(BF16)
        vb_ref[...] = v_ref[...].astype(BF16)

    qb_ref[...] = q_ref[...].astype(BF16)
    acc_ref[...] = jnp.zeros_like(acc_ref)
    c_ref[...] = jnp.zeros_like(c_ref)

    def sweep(j, valid):
        off = pl.multiple_of(j * SB_BLOCK, SB_BLOCK)
        lanes = [slice(h * HEAD_DIM, (h + 1) * HEAD_DIM) for h in range(n_heads)]
        zs = [_dot_nt(qb_ref[:, ln], kb_ref[pl.ds(off, SB_BLOCK), ln]) * (HEAD_DIM ** -0.5) + bias_ref[:, ln]
              for ln in lanes]
        sps = [_softplus(z) for z in zs]
        cums = [_suffix_sums(sp, valid, suf_ref[...]) for sp in sps]
        for ln, z, sp, cum in zip(lanes, zs, sps, cums):
            w = jnp.exp(z - sp + cum[:, :SB_BLOCK] + c_ref[:, ln])
            if valid is not None:
                w = jnp.where(valid, w, 0.0)
            acc_ref[:, ln] += _dot(w.astype(BF16), vb_ref[pl.ds(off, SB_BLOCK), ln])
            c_ref[:, ln] += cum[:, SB_BLOCK:]

    row = lax.broadcasted_iota(jnp.int32, (SB_BLOCK, SB_BLOCK), 0)
    col = lax.broadcasted_iota(jnp.int32, (SB_BLOCK, SB_BLOCK), 1)
    sweep(i, col < row)

    def body(jj, carry):
        sweep(i - jj, None)
        return carry

    lax.fori_loop(1, i + 1, body, 0)
    o_ref[...] = acc_ref[...].astype(o_ref.dtype)


def _sb_prompt(proj, n_seq, seq, n_heads, seg0, sb_bias):
    nq = seq // SB_BLOCK
    width = n_heads * HEAD_DIM
    suffix = _suffix_matrix()
    bias_row = jnp.repeat(sb_bias.astype(F32), HEAD_DIM).reshape(1, width)
    kern = functools.partial(_sb_prompt_kernel, n_heads=n_heads)
    return pl.pallas_call(
        kern,
        grid=(n_seq, nq),
        in_specs=[
            pl.BlockSpec((SB_BLOCK, width), lambda n, i: (n * nq + i, seg0)),
            pl.BlockSpec((seq, width), lambda n, i: (n, seg0 + 1)),
            pl.BlockSpec((seq, width), lambda n, i: (n, seg0 + 2)),
            pl.BlockSpec((1, width), lambda n, i: (0, 0)),
            pl.BlockSpec(suffix.shape, lambda n, i: (0, 0)),
        ],
        out_specs=pl.BlockSpec((SB_BLOCK, width), lambda n, i: (n * nq + i, 0)),
        out_shape=jax.ShapeDtypeStruct((n_seq * seq, width), BF16),
        scratch_shapes=[
            pltpu.VMEM((SB_BLOCK, width), BF16),
            pltpu.VMEM((seq, width), BF16),
            pltpu.VMEM((seq, width), BF16),
            pltpu.VMEM((SB_BLOCK, width), F32),
            pltpu.VMEM((SB_BLOCK, width), F32),
        ],
        compiler_params=_cparams(("parallel", "arbitrary")),
        name="sb_prompt",
    )(proj, proj, proj, bias_row, suffix)


Q_TILE = 8


def _sb_sample_kernel(pt_ref, p_ref, *refs, n_heads, ts, n_pages, sq0):
    k_refs, v_refs = refs[:n_pages], refs[n_pages:2 * n_pages]
    bias_ref, suf_ref, o_ref, q8, knew, vnew, z_scr, w_scr = refs[2 * n_pages:]
    width = n_heads * HEAD_DIM
    grp = n_heads * Q_TILE

    q8[...] = jnp.zeros_like(q8)
    knew[...] = jnp.zeros_like(knew)
    vnew[...] = jnp.zeros_like(vnew)
    for h in range(n_heads):
        c0 = sq0 + h * HEAD_DIM
        q8[h, 0:ts, :] = p_ref[0, :, c0:c0 + HEAD_DIM]
        knew[h, 0:ts, :] = p_ref[0, :, c0 + width:c0 + width + HEAD_DIM]
        vnew[h, 0:ts, :] = p_ref[0, :, c0 + 2 * width:c0 + 2 * width + HEAD_DIM]

    def tile_of(page_refs, own, p, h):
        return (page_refs[p][h] if p < n_pages else own[h]).astype(BF16)

    for p in range(n_pages + 1):
        for h in range(n_heads):
            r0 = (p * n_heads + h) * Q_TILE
            z = _dot_nt(q8[h].astype(BF16), tile_of(k_refs, knew, p, h))
            z_scr[r0:r0 + Q_TILE, :] = z * (HEAD_DIM ** -0.5) + bias_ref[h]

    z = z_scr[...]
    row = lax.broadcasted_iota(jnp.int32, z.shape, 0)
    col = lax.broadcasted_iota(jnp.int32, z.shape, 1)
    valid = (row < n_pages * grp) | (col < jnp.bitwise_and(row, Q_TILE - 1))
    sp = _softplus(z)
    cum = _suffix_sums(sp, valid, suf_ref[...])
    tot = cum[:, SB_BLOCK:]
    later = [None] * (n_pages + 1)
    run = jnp.zeros((grp, SB_BLOCK), F32)
    for p in reversed(range(n_pages + 1)):
        later[p] = run
        run = run + tot[p * grp:(p + 1) * grp]
    c = jnp.concatenate(later, axis=0)
    w_scr[...] = jnp.where(valid, jnp.exp(z - sp + cum[:, :SB_BLOCK] + c), 0.0)

    for h in range(n_heads):
        acc = jnp.zeros((Q_TILE, HEAD_DIM), F32)
        for p in range(n_pages + 1):
            r0 = (p * n_heads + h) * Q_TILE
            acc = acc + _dot(w_scr[r0:r0 + Q_TILE, :].astype(BF16), tile_of(v_refs, vnew, p, h))
        o_ref[0, :, h * HEAD_DIM:(h + 1) * HEAD_DIM] = acc[0:ts]


def _sb_sample(proj3, cache_k, cache_v, layer, page_table, n_heads, sq0, sb_bias):
    nb, ts, n_cols = proj3.shape
    n_pages = page_table.shape[1]
    page = cache_k.shape[3]
    assert page == SB_BLOCK and ts < Q_TILE
    width = n_heads * HEAD_DIM
    rows = (n_pages + 1) * n_heads * Q_TILE
    suffix = _suffix_matrix()
    bias = jnp.broadcast_to(sb_bias.astype(F32)[:, None, None], (n_heads, 1, HEAD_DIM))
    kern = functools.partial(_sb_sample_kernel, n_heads=n_heads, ts=ts, n_pages=n_pages, sq0=sq0)
    page_spec = lambda p: pl.BlockSpec((None, None, n_heads, page, HEAD_DIM),
                                       lambda b, pt: (layer, pt[b * n_pages + p], 0, 0, 0))
    grid_spec = pltpu.PrefetchScalarGridSpec(
        num_scalar_prefetch=1,
        grid=(nb,),
        in_specs=(
            [pl.BlockSpec((1, ts, n_cols), lambda b, pt: (b, 0, 0))]
            + [page_spec(p) for p in range(n_pages)] * 2
            + [pl.BlockSpec(bias.shape, lambda b, pt: (0, 0, 0)),
               pl.BlockSpec(suffix.shape, lambda b, pt: (0, 0))]
        ),
        out_specs=pl.BlockSpec((1, ts, width), lambda b, pt: (b, 0, 0)),
        scratch_shapes=[
            pltpu.VMEM((n_heads, Q_TILE, HEAD_DIM), F32),
            pltpu.VMEM((n_heads, page, HEAD_DIM), F32),
            pltpu.VMEM((n_heads, page, HEAD_DIM), F32),
            pltpu.VMEM((rows, SB_BLOCK), F32),
            pltpu.VMEM((rows, SB_BLOCK), F32),
        ],
    )
    return pl.pallas_call(
        kern,
        grid_spec=grid_spec,
        out_shape=jax.ShapeDtypeStruct((nb, ts, width), F32),
        compiler_params=_cparams(("parallel",)),
        name="sb_sample",
    )(page_table.reshape(-1), proj3, *([cache_k] * n_pages), *([cache_v] * n_pages), bias, suffix)


def _out_proj_kernel(a_ref, b_ref, c_ref, w_ref, x_ref, g_ref, o_ref, *, wa, wb):
    mix = (_dot(a_ref[...].astype(BF16), w_ref[0:wa, :])
           + _dot(b_ref[...].astype(BF16), w_ref[wa:wa + wb, :])
           + _dot(c_ref[...].astype(BF16), w_ref[wa + wb:, :]))
    o_ref[...] = x_ref[...] + _rms(mix, g_ref[...])


def _out_proj(a, b, c, w, x, g, bm):
    rows, d = x.shape
    wa, wb, wc = a.shape[1], b.shape[1], c.shape[1]
    kern = functools.partial(_out_proj_kernel, wa=wa, wb=wb)
    return pl.pallas_call(
        kern,
        grid=(rows // bm,),
        in_specs=[
            pl.BlockSpec((bm, wa), lambda i: (i, 0)),
            pl.BlockSpec((bm, wb), lambda i: (i, 0)),
            pl.BlockSpec((bm, wc), lambda i: (i, 0)),
            pl.BlockSpec(w.shape, lambda i: (0, 0)),
            pl.BlockSpec((bm, d), lambda i: (i, 0)),
            pl.BlockSpec((1, d), lambda i: (0, 0)),
        ],
        out_specs=pl.BlockSpec((bm, d), lambda i: (i, 0)),
        out_shape=jax.ShapeDtypeStruct((rows, d), F32),
        compiler_params=_cparams(("parallel",)),
        name="out_proj",
    )(a, b, c, w, x, g.reshape(1, d))


def _ffn_prompt_kernel(x_ref, gpre_ref, wu_ref, wg_ref, wc_ref, bc_ref, wd_ref, gpost_ref,
                       o_ref, tail_ref, h_ref, acc_ref, carry_ref, *, blocks_per_seq):
    i, j = pl.program_id(0), pl.program_id(1)
    bm = x_ref.shape[0]

    @pl.when(j == 0)
    def _():
        h_ref[...] = _rms(x_ref[...], gpre_ref[...]).astype(BF16)
        acc_ref[...] = jnp.zeros_like(acc_ref)

    @pl.when(lax.rem(i, blocks_per_seq) == 0)
    def _():
        carry_ref[j] = jnp.zeros(carry_ref.shape[1:], F32)

    h = h_ref[...]
    u = _dot(h, wu_ref[...])
    g = _dot(h, wg_ref[...])
    prev = carry_ref[j]
    row = lax.broadcasted_iota(jnp.int32, g.shape, 0)
    g1 = jnp.where(row == 0, prev[7:8, :], pltpu.roll(g, 1, axis=0))
    g2 = jnp.where(row == 0, prev[6:7, :], jnp.where(row == 1, prev[7:8, :], pltpu.roll(g, 2, axis=0)))
    gc = wc_ref[0:1, :] * g2 + wc_ref[1:2, :] * g1 + wc_ref[2:3, :] * g + bc_ref[...]
    acc_ref[...] += _dot((_silu(gc) * u).astype(BF16), wd_ref[...])
    tail = g[bm - 8:bm, :]
    carry_ref[j] = tail
    tail_ref[0] = tail

    @pl.when(j == pl.num_programs(1) - 1)
    def _():
        o_ref[...] = x_ref[...] + _rms(acc_ref[...], gpost_ref[...])


def _ffn_sample_kernel(x_ref, gpre_ref, wu_ref, wg_ref, wc_ref, bc_ref, wd_ref, gpost_ref, prev0_ref, prev1_ref,
                       o_ref, tail0_ref, tail1_ref, h_ref, acc_ref, g_scr, u_scr, act_scr, *, nb, ts):
    j = pl.program_id(0)
    n_prev = FFN_K - 1
    prev_refs = (prev0_ref, prev1_ref)
    tail_refs = (tail0_ref, tail1_ref)
    n_chunks = g_scr.shape[0]

    @pl.when(j == 0)
    def _():
        h_ref[...] = _rms(x_ref[...], gpre_ref[...]).astype(BF16)
        acc_ref[...] = jnp.zeros_like(acc_ref)

    h = h_ref[...]
    u = _dot(h, wu_ref[...])
    g = _dot(h, wg_ref[...])
    for c in range(n_chunks):
        lanes = slice(c * HEAD_DIM, (c + 1) * HEAD_DIM)
        u_scr[c] = u[:, lanes]
        g_scr[c] = g[:, lanes]

    for c in range(n_chunks):
        lanes = slice(c * HEAD_DIM, (c + 1) * HEAD_DIM)

        def g_full(t):
            if t < n_prev:
                return prev_refs[t][:, lanes]
            return g_scr[c, pl.ds(t - n_prev, nb, stride=ts), :]

        for t in range(ts):
            gc = bc_ref[:, lanes]
            for k in range(FFN_K):
                gc = gc + wc_ref[k:k + 1, lanes] * g_full(t + k)
            act_scr[c, pl.ds(t, nb, stride=ts), :] = _silu(gc) * u_scr[c, pl.ds(t, nb, stride=ts), :]
        for t in range(n_prev):
            tail_refs[t][:, lanes] = g_full(ts + t)
        acc_ref[...] += _dot(act_scr[c].astype(BF16), wd_ref[lanes, :])

    @pl.when(j == pl.num_programs(0) - 1)
    def _():
        o_ref[...] = x_ref[...] + _rms(acc_ref[...], gpost_ref[...])


def _ffn_prompt(x, gpre, w_up, w_fconv, b_fconv, w_down, gpost, seq, bm, tf):
    rows, d = x.shape
    d_ff = w_down.shape[0]
    nf = d_ff // tf
    kern = functools.partial(_ffn_prompt_kernel, blocks_per_seq=seq // bm)
    return pl.pallas_call(
        kern,
        grid=(rows // bm, nf),
        in_specs=[
            pl.BlockSpec((bm, d), lambda i, j: (i, 0)),
            pl.BlockSpec((1, d), lambda i, j: (0, 0)),
            pl.BlockSpec((d, tf), lambda i, j: (0, j)),
            pl.BlockSpec((d, tf), lambda i, j: (0, nf + j)),
            pl.BlockSpec((FFN_K, tf), lambda i, j: (0, j)),
            pl.BlockSpec((1, tf), lambda i, j: (0, j)),
            pl.BlockSpec((tf, d), lambda i, j: (j, 0)),
            pl.BlockSpec((1, d), lambda i, j: (0, 0)),
        ],
        out_specs=[
            pl.BlockSpec((bm, d), lambda i, j: (i, 0)),
            pl.BlockSpec((1, 8, tf), lambda i, j: (i, 0, j)),
        ],
        out_shape=[
            jax.ShapeDtypeStruct((rows, d), F32),
            jax.ShapeDtypeStruct((rows // bm, 8, d_ff), F32),
        ],
        scratch_shapes=[
            pltpu.VMEM((bm, d), BF16),
            pltpu.VMEM((bm, d), F32),
            pltpu.VMEM((nf, 8, tf), F32),
        ],
        compiler_params=_cparams(("arbitrary", "arbitrary")),
        name="ffn_prompt",
    )(x, gpre.reshape(1, d), w_up, w_up, w_fconv, b_fconv.reshape(1, -1), w_down, gpost.reshape(1, d))


def _ffn_sample(x, gpre, w_up, w_fconv, b_fconv, w_down, gpost, prev, nb, ts, tf):
    rows, d = x.shape
    d_ff = w_down.shape[0]
    nf = d_ff // tf
    kern = functools.partial(_ffn_sample_kernel, nb=nb, ts=ts)
    chunk = pltpu.VMEM((tf // HEAD_DIM, rows, HEAD_DIM), F32)
    return pl.pallas_call(
        kern,
        grid=(nf,),
        in_specs=[
            pl.BlockSpec((rows, d), lambda j: (0, 0)),
            pl.BlockSpec((1, d), lambda j: (0, 0)),
            pl.BlockSpec((d, tf), lambda j: (0, j)),
            pl.BlockSpec((d, tf), lambda j: (0, nf + j)),
            pl.BlockSpec((FFN_K, tf), lambda j: (0, j)),
            pl.BlockSpec((1, tf), lambda j: (0, j)),
            pl.BlockSpec((tf, d), lambda j: (j, 0)),
            pl.BlockSpec((1, d), lambda j: (0, 0)),
            pl.BlockSpec((nb, tf), lambda j: (0, j)),
            pl.BlockSpec((nb, tf), lambda j: (0, nf + j)),
        ],
        out_specs=[
            pl.BlockSpec((rows, d), lambda j: (0, 0)),
            pl.BlockSpec((nb, tf), lambda j: (0, j)),
            pl.BlockSpec((nb, tf), lambda j: (0, j)),
        ],
        out_shape=[
            jax.ShapeDtypeStruct((rows, d), F32),
            jax.ShapeDtypeStruct((nb, d_ff), F32),
            jax.ShapeDtypeStruct((nb, d_ff), F32),
        ],
        scratch_shapes=[pltpu.VMEM((rows, d), BF16), pltpu.VMEM((rows, d), F32), chunk, chunk, chunk],
        compiler_params=_cparams(("arbitrary",)),
        name="ffn_sample",
    )(x, gpre.reshape(1, d), w_up, w_up, w_fconv, b_fconv.reshape(1, -1), w_down, gpost.reshape(1, d), prev, prev)


def _pick(total, want):
    if total <= want:
        return total
    best = None
    for cand in range(128, want + 1, 128):
        if total % cand == 0:
            best = cand
    assert best is not None, (total, want)
    return best


def kernel(x_prompt, x_sample, cache_k, cache_v, state_conv, state_ret, state_ffn, page_table, g_pre_mix, w_in, w_dw, b_dw, gn_conv_g, gn_conv_b, gn_ret_g, sb_bias, w_out, g_post_mix, g_pre_ffn, w_up, w_fconv, b_fconv, w_down, g_post_ffn):
    bp, tp, d = x_prompt.shape
    bs, ts, _ = x_sample.shape
    depth = w_in.shape[0]
    conv_w = w_dw.shape[2]
    n_ret = gn_ret_g.shape[1] // HEAD_DIM
    n_sb = sb_bias.shape[1]
    d_ff = w_down.shape[1]
    past_len = page_table.shape[1] * cache_k.shape[2]
    assert conv_w % HEAD_DIM == 0 and tp % RET_CHUNK == 0 and tp % SB_BLOCK == 0 and ts < 8

    ret_w, sb_w = n_ret * HEAD_DIM, n_sb * HEAD_DIM
    glu_w = 2 * conv_w
    sq0 = 4 * ret_w
    sk0, sv0 = sq0 + sb_w, sq0 + 2 * sb_w
    c_conv = (4 * ret_w + 3 * sb_w) // HEAD_DIM
    assert sq0 % sb_w == 0
    page = cache_k.shape[2]
    cache_kt = jnp.transpose(cache_k, (0, 1, 3, 2, 4))
    cache_vt = jnp.transpose(cache_v, (0, 1, 3, 2, 4))

    bm_p = _pick(bp * tp, 512)
    bm_seq = _pick(tp, 512)
    bn = _pick(w_in.shape[2], 1280)
    tf = _pick(d_ff, 512)

    cos_p, sin_p = _rope_tables(jnp.arange(tp))
    cos_s, sin_s = _rope_tables(past_len + jnp.arange(ts))

    xp = x_prompt.reshape(bp * tp, d)
    xs = x_sample.reshape(bs * ts, d)
    outs = [[] for _ in range(10)]
    for l in range(depth):
        w_in_b = jnp.concatenate([w_in[l][:, glu_w:], w_in[l][:, :glu_w]], axis=1).astype(BF16)
        w_out_b = w_out[l].astype(BF16)
        w_up_b = w_up[l].astype(BF16)
        w_down_b = w_down[l].astype(BF16)

        proj = _in_proj(xp, g_pre_mix[l], w_in_b, bm_p, bn)
        a_out, conv_tail = _conv_prompt(proj, c_conv, bp, tp, w_dw[l], b_dw[l], gn_conv_g[l], gn_conv_b[l], bm_seq)
        b_out, ret_state = _ret_prompt(proj, bp, tp, n_ret, 0, cos_p, sin_p, gn_ret_g[l])
        c_out = _sb_prompt(proj, bp, tp, n_sb, sq0 // sb_w, sb_bias[l])
        xp = _out_proj(a_out, b_out, c_out, w_out_b, xp, g_post_mix[l], _pick(bp * tp, 256))
        xp, ffn_tail = _ffn_prompt(xp, g_pre_ffn[l], w_up_b, w_fconv[l], b_fconv[l], w_down_b, g_post_ffn[l],
                                   tp, bm_seq, tf)
        outs[0].append(conv_tail[:, CONV_PAD - (CONV_K - 1):, :])
        outs[1].append(ret_state)
        outs[2].append(proj[:, sk0:sk0 + sb_w].reshape(bp, tp // page, page, n_sb, HEAD_DIM))
        outs[3].append(proj[:, sv0:sv0 + sb_w].reshape(bp, tp // page, page, n_sb, HEAD_DIM))
        outs[4].append(ffn_tail.reshape(bp, tp // bm_seq, 8, d_ff)[:, -1, 8 - (FFN_K - 1):, :])

        proj_s = _in_proj(xs, g_pre_mix[l], w_in_b, bs * ts, bn)
        proj_s3 = proj_s.reshape(bs, ts, -1)
        a_s, a_new = _conv_sample(proj_s, c_conv, state_conv[l].reshape(bs * (CONV_K - 1), conv_w), bs, ts,
                                  w_dw[l], b_dw[l], gn_conv_g[l], gn_conv_b[l])
        b_s, ret_s = _ret_sample(proj_s3, state_ret[l], n_ret, 0, cos_s, sin_s, gn_ret_g[l])
        c_s = _sb_sample(proj_s3, cache_kt, cache_vt, l, page_table, n_sb, sq0, sb_bias[l])
        xs = _out_proj(a_s, b_s.reshape(bs * ts, -1), c_s.reshape(bs * ts, -1), w_out_b, xs, g_post_mix[l], bs * ts)
        xs, ffn_t0, ffn_t1 = _ffn_sample(xs, g_pre_ffn[l], w_up_b, w_fconv[l], b_fconv[l], w_down_b, g_post_ffn[l],
                                         state_ffn[l].reshape(bs, (FFN_K - 1) * d_ff), bs, ts, tf)
        conv_full = jnp.concatenate([state_conv[l], a_new.reshape(bs, ts, conv_w)], axis=1)
        outs[5].append(conv_full[:, -(CONV_K - 1):, :])
        outs[6].append(ret_s)
        outs[7].append(proj_s3[:, :, sk0:sk0 + sb_w].reshape(bs, ts, n_sb, HEAD_DIM))
        outs[8].append(proj_s3[:, :, sv0:sv0 + sb_w].reshape(bs, ts, n_sb, HEAD_DIM))
        outs[9].append(jnp.stack([ffn_t0, ffn_t1], axis=1))

    stacked = [jnp.stack(o) for o in outs]
    return (xp.reshape(bp, tp, d), xs.reshape(bs, ts, d), *stacked)
```

```python
import functools

import numpy as np
import jax
import jax.numpy as jnp
from jax import lax
from jax.experimental import pallas as pl
from jax.experimental.pallas import tpu as pltpu

F32 = jnp.float32
BF16 = jnp.bfloat16

EPS = 1e-6
HEAD_DIM = 128
CONV_K = 31
CONV_PAD = 32
FFN_K = 3
ROPE_BASE = 10000.0
RET_CHUNK = 128
V7X_VMEM_BYTES = 64 * 1024 * 1024
VMEM_LIMIT = V7X_VMEM_BYTES - 8 * 1024 * 1024


def _cparams(sem):
    return pltpu.CompilerParams(dimension_semantics=sem, vmem_limit_bytes=VMEM_LIMIT)


def _rms(x, g):
    ms = jnp.mean(x * x, axis=-1, keepdims=True)
    return x * lax.rsqrt(ms + EPS) * g


def _group_norm(y):
    mu = jnp.mean(y, axis=-1, keepdims=True)
    d = y - mu
    var = jnp.mean(d * d, axis=-1, keepdims=True)
    return d * lax.rsqrt(var + EPS)


def _silu(x):
    return x * jax.nn.sigmoid(x)


def _softplus(z):
    return jnp.maximum(z, 0.0) + jnp.log(1.0 + jnp.exp(-jnp.abs(z)))


def _dot(a, b):
    return jnp.dot(a, b, preferred_element_type=F32)


def _dot_nt(a, b):
    return lax.dot_general(a, b, (((1,), (1,)), ((), ())), preferred_element_type=F32)


def _dot_tn(a, b):
    return lax.dot_general(a, b, (((0,), (0,)), ((), ())), preferred_element_type=F32)


def _in_proj_kernel(x_ref, g_ref, w_ref, o_ref, h_ref):
    @pl.when(pl.program_id(1) == 0)
    def _():
        h_ref[...] = _rms(x_ref[...], g_ref[...]).astype(BF16)

    o_ref[...] = _dot(h_ref[...], w_ref[...])


def _in_proj(x, g, w, layer, bm, bn):
    rows, d = x.shape
    n = w.shape[2]
    return pl.pallas_call(
        _in_proj_kernel,
        grid=(rows // bm, n // bn),
        in_specs=[
            pl.BlockSpec((bm, d), lambda i, j: (i, 0)),
            pl.BlockSpec((1, d), lambda i, j: (0, 0)),
            pl.BlockSpec((None, d, bn), lambda i, j: (layer, 0, j)),
        ],
        out_specs=pl.BlockSpec((bm, bn), lambda i, j: (i, j)),
        out_shape=jax.ShapeDtypeStruct((rows, n), F32),
        scratch_shapes=[pltpu.VMEM((bm, d), BF16)],
        compiler_params=_cparams(("parallel", "arbitrary")),
        name="in_proj",
    )(x, g.reshape(1, d), w)


def _conv_prompt_kernel(val_ref, gate_ref, w_ref, b_ref, gg_ref, gb_ref, o_ref, st_ref, abuf):
    tb = val_ref.shape[0]
    sub = min(tb, 128)

    @pl.when(pl.program_id(2) == 0)
    def _():
        abuf[0:CONV_PAD, :] = jnp.zeros((CONV_PAD, HEAD_DIM), F32)

    abuf[CONV_PAD:CONV_PAD + tb, :] = val_ref[...] * jax.nn.sigmoid(gate_ref[...])
    lead = CONV_PAD - (CONV_K - 1)
    for r0 in range(0, tb, sub):
        y = jnp.broadcast_to(b_ref[...], (sub, HEAD_DIM))
        for k in range(CONV_K):
            y = y + w_ref[k:k + 1, :] * abuf[r0 + k + lead:r0 + k + lead + sub, :]
        yn = _group_norm(y) * gg_ref[...] + gb_ref[...]
        o_ref[r0:r0 + sub, :] = _silu(yn).astype(o_ref.dtype)
    tail = abuf[tb:tb + CONV_PAD, :]
    st_ref[0] = tail
    abuf[0:CONV_PAD, :] = tail


def _conv_prompt(proj, col0, n_seq, seq, w_dw, b_dw, gn_g, gn_b, tb):
    conv_w = w_dw.shape[1]
    groups = conv_w // HEAD_DIM
    nt = seq // tb
    return pl.pallas_call(
        _conv_prompt_kernel,
        grid=(n_seq, groups, nt),
        in_specs=[
            pl.BlockSpec((tb, HEAD_DIM), lambda n, g, t: (n * nt + t, col0 + g)),
            pl.BlockSpec((tb, HEAD_DIM), lambda n, g, t: (n * nt + t, col0 + groups + g)),
            pl.BlockSpec((CONV_K, HEAD_DIM), lambda n, g, t: (0, g)),
            pl.BlockSpec((1, HEAD_DIM), lambda n, g, t: (0, g)),
            pl.BlockSpec((1, HEAD_DIM), lambda n, g, t: (0, g)),
            pl.BlockSpec((1, HEAD_DIM), lambda n, g, t: (0, g)),
        ],
        out_specs=[
            pl.BlockSpec((tb, HEAD_DIM), lambda n, g, t: (n * nt + t, g)),
            pl.BlockSpec((1, CONV_PAD, HEAD_DIM), lambda n, g, t: (n, 0, g)),
        ],
        out_shape=[
            jax.ShapeDtypeStruct((n_seq * seq, conv_w), BF16),
            jax.ShapeDtypeStruct((n_seq, CONV_PAD, conv_w), F32),
        ],
        scratch_shapes=[pltpu.VMEM((CONV_PAD + tb, HEAD_DIM), F32)],
        compiler_params=_cparams(("parallel", "parallel", "arbitrary")),
        name="conv_prompt",
    )(proj, proj, w_dw, b_dw.reshape(1, -1), gn_g.reshape(1, -1), gn_b.reshape(1, -1))


def _conv_sample_kernel(val_ref, gate_ref, st_ref, w_ref, b_ref, gg_ref, gb_ref, o_ref, a_ref, *, nb, ts, n_prev):
    a_ref[...] = val_ref[...] * jax.nn.sigmoid(gate_ref[...])

    def a_full(j):
        if j < n_prev:
            return st_ref[pl.ds(j, nb, stride=n_prev), :]
        return a_ref[pl.ds(j - n_prev, nb, stride=ts), :]

    for t in range(ts):
        y = jnp.broadcast_to(b_ref[...], (nb, HEAD_DIM))
        for k in range(CONV_K):
            y = y + w_ref[k:k + 1, :] * a_full(t + k)
        yn = _group_norm(y) * gg_ref[...] + gb_ref[...]
        o_ref[pl.ds(t, nb, stride=ts), :] = _silu(yn)


def _conv_sample(proj, col0, state2d, nb, ts, w_dw, b_dw, gn_g, gn_b):
    conv_w = w_dw.shape[1]
    groups = conv_w // HEAD_DIM
    n_prev = CONV_K - 1
    rows = nb * ts
    kern = functools.partial(_conv_sample_kernel, nb=nb, ts=ts, n_prev=n_prev)
    return pl.pallas_call(
        kern,
        grid=(groups,),
        in_specs=[
            pl.BlockSpec((rows, HEAD_DIM), lambda g: (0, col0 + g)),
            pl.BlockSpec((rows, HEAD_DIM), lambda g: (0, col0 + groups + g)),
            pl.BlockSpec((nb * n_prev, HEAD_DIM), lambda g: (0, g)),
            pl.BlockSpec((CONV_K, HEAD_DIM), lambda g: (0, g)),
            pl.BlockSpec((1, HEAD_DIM), lambda g: (0, g)),
            pl.BlockSpec((1, HEAD_DIM), lambda g: (0, g)),
            pl.BlockSpec((1, HEAD_DIM), lambda g: (0, g)),
        ],
        out_specs=[
            pl.BlockSpec((rows, HEAD_DIM), lambda g: (0, g)),
            pl.BlockSpec((rows, HEAD_DIM), lambda g: (0, g)),
        ],
        out_shape=[
            jax.ShapeDtypeStruct((rows, conv_w), F32),
            jax.ShapeDtypeStruct((rows, conv_w), F32),
        ],
        compiler_params=_cparams(("parallel",)),
        name="conv_sample",
    )(proj, proj, state2d, w_dw, b_dw.reshape(1, -1), gn_g.reshape(1, -1), gn_b.reshape(1, -1))


def _rope(x, cos2, sin_signed):
    return x * cos2 + pltpu.roll(x, HEAD_DIM // 2, axis=1) * sin_signed


def _ret_prompt_kernel(q_ref, k_ref, v_ref, g_ref, cos_ref, sin_ref, dec_ref, xi_ref, zeta_ref, gl_ref, gn_ref,
                       o_ref, so_ref, s_ref):
    @pl.when(pl.program_id(1) == 0)
    def _():
        s_ref[...] = jnp.zeros_like(s_ref)

    cos2, sin_s = cos_ref[...], sin_ref[...]
    heads = range(s_ref.shape[0])
    lanes = [slice(h * HEAD_DIM, (h + 1) * HEAD_DIM) for h in heads]
    qs = [_rope(q_ref[:, ln], cos2, sin_s).astype(BF16) for ln in lanes]
    ks = [_rope(k_ref[:, ln], cos2, sin_s) * (HEAD_DIM ** -0.5) for ln in lanes]
    vs = [v_ref[:, ln].astype(BF16) for ln in lanes]
    scores = [_dot_nt(qs[h], ks[h].astype(BF16)) * dec_ref[h] for h in heads]
    cross = [_dot(qs[h], s_ref[h].astype(BF16)) * xi_ref[h] for h in heads]
    kv = [_dot_tn((ks[h] * zeta_ref[h]).astype(BF16), vs[h]) for h in heads]
    inner = [_dot(scores[h].astype(BF16), vs[h]) for h in heads]
    for h in heads:
        s_new = gl_ref[h] * s_ref[h] + kv[h]
        s_ref[h] = s_new
        so_ref[0, h] = s_new
        r = inner[h] + cross[h]
        o_ref[:, lanes[h]] = (_group_norm(r) * gn_ref[:, lanes[h]] * _silu(g_ref[:, lanes[h]])).astype(o_ref.dtype)


def _ret_consts(length, n_heads):
    log_gamma = np.log1p(-np.exp2(-5.0 - np.arange(n_heads, dtype=np.float64)))
    idx = np.arange(RET_CHUNK, dtype=np.float64)
    diff = idx[:, None] - idx[None, :]
    live = (diff >= 0) & (idx[:, None] < length) & (idx[None, :] < length)
    dec = np.where(live[None], np.exp(np.maximum(diff, 0.0)[None] * log_gamma[:, None, None]), 0.0)
    xi = np.exp((idx[None, :] + 1.0) * log_gamma[:, None])
    zeta = np.where(idx[None, :] < length, np.exp((length - 1.0 - idx)[None, :] * log_gamma[:, None]), 0.0)
    gl = np.exp(length * log_gamma)
    bc = lambda a: jnp.asarray(np.broadcast_to(a[:, :, None], (n_heads, RET_CHUNK, HEAD_DIM)), F32)
    return (jnp.asarray(dec, F32), bc(xi), bc(zeta),
            jnp.asarray(np.broadcast_to(gl[:, None, None], (n_heads, 1, HEAD_DIM)), F32))


def _rope_tables(pos):
    half = HEAD_DIM // 2
    inv = ROPE_BASE ** (-jnp.arange(half, dtype=F32) / half)
    ang = pos.astype(F32)[:, None] * inv[None, :]
    cos, sin = jnp.cos(ang), jnp.sin(ang)
    return jnp.concatenate([cos, cos], axis=1), jnp.concatenate([-sin, sin], axis=1)


def _ret_prompt(proj, n_seq, seq, n_heads, seg0, cos2, sin_s, gn_g):
    nc = seq // RET_CHUNK
    width = n_heads * HEAD_DIM
    dec, xi, zeta, gl = _ret_consts(RET_CHUNK, n_heads)
    blk = lambda off: pl.BlockSpec((RET_CHUNK, width), lambda n, c: (n * nc + c, seg0 + off))
    full = lambda a: pl.BlockSpec(a.shape, lambda n, c: (0,) * a.ndim)
    return pl.pallas_call(
        _ret_prompt_kernel,
        grid=(n_seq, nc),
        in_specs=[
            blk(0), blk(1), blk(2), blk(3),
            pl.BlockSpec((RET_CHUNK, HEAD_DIM), lambda n, c: (c, 0)),
            pl.BlockSpec((RET_CHUNK, HEAD_DIM), lambda n, c: (c, 0)),
            full(dec), full(xi), full(zeta), full(gl),
            pl.BlockSpec((1, width), lambda n, c: (0, 0)),
        ],
        out_specs=[
            pl.BlockSpec((RET_CHUNK, width), lambda n, c: (n * nc + c, 0)),
            pl.BlockSpec((1, n_heads, HEAD_DIM, HEAD_DIM), lambda n, c: (n, 0, 0, 0)),
        ],
        out_shape=[
            jax.ShapeDtypeStruct((n_seq * seq, width), BF16),
            jax.ShapeDtypeStruct((n_seq, n_heads, HEAD_DIM, HEAD_DIM), F32),
        ],
        scratch_shapes=[pltpu.VMEM((n_heads, HEAD_DIM, HEAD_DIM), F32)],
        compiler_params=_cparams(("parallel", "arbitrary")),
        name="ret_prompt",
    )(proj, proj, proj, proj, cos2, sin_s, dec, xi, zeta, gl, gn_g.reshape(1, -1))


def _ret_sample_kernel(p_ref, s_ref, cos_ref, sin_ref, dec_ref, xi_ref, zeta_ref, gl_ref, gn_ref, *refs,
                       n_heads, col0, ts):
    o_ref, so_ref, qp, kp, kzp, vp = refs[-6:]
    cos2, sin_s = cos_ref[...], sin_ref[...]
    heads = range(n_heads)
    cols = lambda off, h: slice((col0 + off * n_heads + h) * HEAD_DIM, (col0 + off * n_heads + h + 1) * HEAD_DIM)
    qp[...] = jnp.zeros_like(qp)
    kp[...] = jnp.zeros_like(kp)
    kzp[...] = jnp.zeros_like(kzp)
    vp[...] = jnp.zeros_like(vp)
    for h in heads:
        k = _rope(p_ref[0, :, cols(1, h)], cos2, sin_s) * (HEAD_DIM ** -0.5)
        qp[h, 0:ts, :] = _rope(p_ref[0, :, cols(0, h)], cos2, sin_s)
        kp[h, 0:ts, :] = k
        kzp[h, 0:ts, :] = k * zeta_ref[h, 0:ts, :]
        vp[h, 0:ts, :] = p_ref[0, :, cols(2, h)]
    qs = [qp[h].astype(BF16) for h in heads]
    vs = [vp[h].astype(BF16) for h in heads]
    scores = [_dot_nt(qs[h], kp[h].astype(BF16)) * dec_ref[h, 0:8, :] for h in heads]
    cross = [_dot(qs[h], s_ref[0, h].astype(BF16)) * xi_ref[h, 0:8, :] for h in heads]
    kv = [_dot_tn(kzp[h].astype(BF16), vs[h]) for h in heads]
    inner = [_dot(scores[h].astype(BF16), vs[h]) for h in heads]
    for h in heads:
        so_ref[0, h] = gl_ref[h] * s_ref[0, h] + kv[h]
        r = (inner[h] + cross[h])[0:ts]
        gn = gn_ref[:, h * HEAD_DIM:(h + 1) * HEAD_DIM]
        o_ref[0, :, h * HEAD_DIM:(h + 1) * HEAD_DIM] = _group_norm(r) * gn * _silu(p_ref[0, :, cols(3, h)])


def _ret_sample(proj3, state, layer, new_state, n_heads, col0, cos2, sin_s, gn_g):
    nb, ts, n_cols = proj3.shape
    dec, xi, zeta, gl = _ret_consts(ts, n_heads)
    kern = functools.partial(_ret_sample_kernel, n_heads=n_heads, col0=col0, ts=ts)
    full = lambda a: pl.BlockSpec(a.shape, lambda b: (0,) * a.ndim)
    state_spec = pl.BlockSpec((None, 1, n_heads, HEAD_DIM, HEAD_DIM), lambda b: (layer, b, 0, 0, 0))
    earlier = [] if new_state is None else [new_state]
    return pl.pallas_call(
        kern,
        grid=(nb,),
        in_specs=[
            pl.BlockSpec((1, ts, n_cols), lambda b: (b, 0, 0)),
            state_spec,
            full(cos2), full(sin_s), full(dec), full(xi), full(zeta), full(gl),
            pl.BlockSpec((1, n_heads * HEAD_DIM), lambda b: (0, 0)),
        ] + [pl.BlockSpec(memory_space=pl.ANY)] * len(earlier),
        out_specs=[pl.BlockSpec((1, ts, n_heads * HEAD_DIM), lambda b: (b, 0, 0)), state_spec],
        out_shape=[
            jax.ShapeDtypeStruct((nb, ts, n_heads * HEAD_DIM), F32),
            jax.ShapeDtypeStruct(state.shape, F32),
        ],
        input_output_aliases={9 + a: 1 + a for a in range(len(earlier))},
        scratch_shapes=[
            pltpu.VMEM((n_heads, 8, HEAD_DIM), F32),
            pltpu.VMEM((n_heads, RET_CHUNK, HEAD_DIM), F32),
            pltpu.VMEM((n_heads, RET_CHUNK, HEAD_DIM), F32),
            pltpu.VMEM((n_heads, RET_CHUNK, HEAD_DIM), F32),
        ],
        compiler_params=_cparams(("parallel",)),
        name="ret_sample",
    )(proj3, state, cos2, sin_s, dec, xi, zeta, gl, gn_g.reshape(1, -1), *earlier)


SB_BLOCK = 128


def _suffix_matrix():
    j = np.arange(SB_BLOCK)
    strict = (j[:, None] > j[None, :]).astype(np.float32)
    half = np.concatenate([strict, np.ones((SB_BLOCK, SB_BLOCK), np.float32)], axis=1)
    return jnp.asarray(np.concatenate([half, half], axis=0), BF16)


def _suffix_sums(sp, valid, suffix):
    lk = -sp if valid is None else jnp.where(valid, -sp, 0.0)
    hi = lk.astype(BF16)
    lo = (lk - hi.astype(F32)).astype(BF16)
    return _dot(jnp.concatenate([hi, lo], axis=1), suffix)


def _sb_prompt_kernel(q_ref, k_ref, v_ref, bias_ref, suf_ref, *refs, n_heads):
    o_ref, pk_ref, pv_ref, qb_ref, kb_ref, vb_ref, acc_ref, c_ref = refs[-8:]
    i = pl.program_id(1)

    @pl.when(i == 0)
    def _():
        kb_ref[...] = k_ref[...].astype(BF16)
        vb_ref[...] = v_ref[...].astype(BF16)

    qb_ref[...] = q_ref[...].astype(BF16)
    acc_ref[...] = jnp.zeros_like(acc_ref)
    c_ref[...] = jnp.zeros_like(c_ref)

    lanes = [slice(h * HEAD_DIM, (h + 1) * HEAD_DIM) for h in range(n_heads)]
    own = pl.ds(pl.multiple_of(i * SB_BLOCK, SB_BLOCK), SB_BLOCK)
    for h, ln in enumerate(lanes):
        pk_ref[h] = k_ref[own, ln]
        pv_ref[h] = v_ref[own, ln]

    def sweep(blocks, valid):
        offs = [j * SB_BLOCK if isinstance(j, int) else pl.multiple_of(j * SB_BLOCK, SB_BLOCK) for j in blocks]
        zs = [[_dot_nt(qb_ref[:, ln], kb_ref[pl.ds(off, SB_BLOCK), ln]) * (HEAD_DIM ** -0.5) + bias_ref[:, ln]
               for ln in lanes] for off in offs]
        sps = [[_softplus(z) for z in row_z] for row_z in zs]
        cums = [[_suffix_sums(sp, valid, suf_ref[...]) for sp in row_sp] for row_sp in sps]
        for h, ln in enumerate(lanes):
            c = c_ref[:, ln]
            pv = None
            for b, off in enumerate(offs):
                w = jnp.exp(zs[b][h] - sps[b][h] + cums[b][h][:, :SB_BLOCK] + c)
                if valid is not None:
                    w = jnp.where(valid, w, 0.0)
                d = _dot(w.astype(BF16), vb_ref[pl.ds(off, SB_BLOCK), ln])
                pv = d if pv is None else pv + d
                c = c + cums[b][h][:, SB_BLOCK:]
            acc_ref[:, ln] += pv
            c_ref[:, ln] = c

    row = lax.broadcasted_iota(jnp.int32, (SB_BLOCK, SB_BLOCK), 0)
    col = lax.broadcasted_iota(jnp.int32, (SB_BLOCK, SB_BLOCK), 1)
    sweep([i], col < row)

    def body(t, carry):
        j = i - 1 - 2 * t
        sweep([j, j - 1], None)
        return carry

    lax.fori_loop(0, lax.shift_right_logical(i, 1), body, 0)

    @pl.when(jnp.bitwise_and(i, 1) == 1)
    def _():
        sweep([0], None)

    o_ref[...] = acc_ref[...].astype(o_ref.dtype)


def _sb_prompt(proj, n_seq, seq, n_heads, seg0, sb_bias, layer, depth, pages_kv):
    nq = seq // SB_BLOCK
    width = n_heads * HEAD_DIM
    suffix = _suffix_matrix()
    bias_row = jnp.repeat(sb_bias.astype(F32), HEAD_DIM).reshape(1, width)
    kern = functools.partial(_sb_prompt_kernel, n_heads=n_heads)
    pages_shape = jax.ShapeDtypeStruct((depth, n_seq, nq, n_heads, SB_BLOCK, HEAD_DIM), F32)
    pages_spec = pl.BlockSpec((None, None, None, n_heads, SB_BLOCK, HEAD_DIM), lambda n, i: (layer, n, i, 0, 0, 0))
    earlier = [] if pages_kv is None else list(pages_kv)
    return pl.pallas_call(
        kern,
        grid=(n_seq, nq),
        in_specs=[
            pl.BlockSpec((SB_BLOCK, width), lambda n, i: (n * nq + i, seg0)),
            pl.BlockSpec((seq, width), lambda n, i: (n, seg0 + 1)),
            pl.BlockSpec((seq, width), lambda n, i: (n, seg0 + 2)),
            pl.BlockSpec((1, width), lambda n, i: (0, 0)),
            pl.BlockSpec(suffix.shape, lambda n, i: (0, 0)),
        ] + [pl.BlockSpec(memory_space=pl.ANY)] * len(earlier),
        out_specs=[pl.BlockSpec((SB_BLOCK, width), lambda n, i: (n * nq + i, 0)), pages_spec, pages_spec],
        out_shape=[jax.ShapeDtypeStruct((n_seq * seq, width), BF16), pages_shape, pages_shape],
        input_output_aliases={5 + a: 1 + a for a in range(len(earlier))},
        scratch_shapes=[
            pltpu.VMEM((SB_BLOCK, width), BF16),
            pltpu.VMEM((seq, width), BF16),
            pltpu.VMEM((seq, width), BF16),
            pltpu.VMEM((SB_BLOCK, width), F32),
            pltpu.VMEM((SB_BLOCK, width), F32),
        ],
        compiler_params=_cparams(("parallel", "arbitrary")),
        name="sb_prompt",
    )(proj, proj, proj, bias_row, suffix, *earlier)


Q_TILE = 8


def _sb_sample_kernel(pt_ref, p_ref, *refs, n_heads, ts, n_pages, sq0):
    k_refs, v_refs = refs[:n_pages], refs[n_pages:2 * n_pages]
    bias_ref, suf_ref, o_ref, q8, knew, vnew, z_scr, w_scr = refs[2 * n_pages:]
    width = n_heads * HEAD_DIM
    grp = n_heads * Q_TILE

    q8[...] = jnp.zeros_like(q8)
    knew[...] = jnp.zeros_like(knew)
    vnew[...] = jnp.zeros_like(vnew)
    for h in range(n_heads):
        c0 = sq0 + h * HEAD_DIM
        q8[h, 0:ts, :] = p_ref[0, :, c0:c0 + HEAD_DIM]
        knew[h, 0:ts, :] = p_ref[0, :, c0 + width:c0 + width + HEAD_DIM]
        vnew[h, 0:ts, :] = p_ref[0, :, c0 + 2 * width:c0 + 2 * width + HEAD_DIM]

    def tile_of(page_refs, own, p, h):
        return (page_refs[p][h] if p < n_pages else own[h]).astype(BF16)

    for p in range(n_pages + 1):
        for h in range(n_heads):
            r0 = (p * n_heads + h) * Q_TILE
            z = _dot_nt(q8[h].astype(BF16), tile_of(k_refs, knew, p, h))
            z_scr[r0:r0 + Q_TILE, :] = z * (HEAD_DIM ** -0.5) + bias_ref[h]

    z = z_scr[...]
    row = lax.broadcasted_iota(jnp.int32, z.shape, 0)
    col = lax.broadcasted_iota(jnp.int32, z.shape, 1)
    valid = (row < n_pages * grp) | (col < jnp.bitwise_and(row, Q_TILE - 1))
    sp = _softplus(z)
    cum = _suffix_sums(sp, valid, suf_ref[...])
    tot = cum[:, SB_BLOCK:]
    later = [None] * (n_pages + 1)
    run = jnp.zeros((grp, SB_BLOCK), F32)
    for p in reversed(range(n_pages + 1)):
        later[p] = run
        run = run + tot[p * grp:(p + 1) * grp]
    c = jnp.concatenate(later, axis=0)
    w_scr[...] = jnp.where(valid, jnp.exp(z - sp + cum[:, :SB_BLOCK] + c), 0.0)

    for h in range(n_heads):
        acc = jnp.zeros((Q_TILE, HEAD_DIM), F32)
        for p in range(n_pages + 1):
            r0 = (p * n_heads + h) * Q_TILE
            acc = acc + _dot(w_scr[r0:r0 + Q_TILE, :].astype(BF16), tile_of(v_refs, vnew, p, h))
        o_ref[0, :, h * HEAD_DIM:(h + 1) * HEAD_DIM] = acc[0:ts]


def _sb_sample(proj3, cache_k, cache_v, layer, page_table, n_heads, sq0, sb_bias):
    nb, ts, n_cols = proj3.shape
    n_pages = page_table.shape[1]
    page = cache_k.shape[3]
    assert page == SB_BLOCK and ts < Q_TILE
    width = n_heads * HEAD_DIM
    rows = (n_pages + 1) * n_heads * Q_TILE
    suffix = _suffix_matrix()
    bias = jnp.broadcast_to(sb_bias.astype(F32)[:, None, None], (n_heads, 1, HEAD_DIM))
    kern = functools.partial(_sb_sample_kernel, n_heads=n_heads, ts=ts, n_pages=n_pages, sq0=sq0)
    page_spec = lambda p: pl.BlockSpec((None, None, n_heads, page, HEAD_DIM),
                                       lambda b, pt: (layer, pt[b * n_pages + p], 0, 0, 0))
    grid_spec = pltpu.PrefetchScalarGridSpec(
        num_scalar_prefetch=1,
        grid=(nb,),
        in_specs=(
            [pl.BlockSpec((1, ts, n_cols), lambda b, pt: (b, 0, 0))]
            + [page_spec(p) for p in range(n_pages)] * 2
            + [pl.BlockSpec(bias.shape, lambda b, pt: (0, 0, 0)),
               pl.BlockSpec(suffix.shape, lambda b, pt: (0, 0))]
        ),
        out_specs=pl.BlockSpec((1, ts, width), lambda b, pt: (b, 0, 0)),
        scratch_shapes=[
            pltpu.VMEM((n_heads, Q_TILE, HEAD_DIM), F32),
            pltpu.VMEM((n_heads, page, HEAD_DIM), F32),
            pltpu.VMEM((n_heads, page, HEAD_DIM), F32),
            pltpu.VMEM((rows, SB_BLOCK), F32),
            pltpu.VMEM((rows, SB_BLOCK), F32),
        ],
    )
    return pl.pallas_call(
        kern,
        grid_spec=grid_spec,
        out_shape=jax.ShapeDtypeStruct((nb, ts, width), F32),
        compiler_params=_cparams(("parallel",)),
        name="sb_sample",
    )(page_table.reshape(-1), proj3, *([cache_k] * n_pages), *([cache_v] * n_pages), bias, suffix)


def _out_proj_kernel(a_ref, b_ref, c_ref, w_ref, x_ref, g_ref, o_ref, *, wa, wb):
    mix = (_dot(a_ref[...].astype(BF16), w_ref[0:wa, :])
           + _dot(b_ref[...].astype(BF16), w_ref[wa:wa + wb, :])
           + _dot(c_ref[...].astype(BF16), w_ref[wa + wb:, :]))
    o_ref[...] = x_ref[...] + _rms(mix, g_ref[...])


def _out_proj(a, b, c, w, layer, x, g, bm):
    rows, d = x.shape
    wa, wb, wc = a.shape[1], b.shape[1], c.shape[1]
    kern = functools.partial(_out_proj_kernel, wa=wa, wb=wb)
    return pl.pallas_call(
        kern,
        grid=(rows // bm,),
        in_specs=[
            pl.BlockSpec((bm, wa), lambda i: (i, 0)),
            pl.BlockSpec((bm, wb), lambda i: (i, 0)),
            pl.BlockSpec((bm, wc), lambda i: (i, 0)),
            pl.BlockSpec((None,) + w.shape[1:], lambda i: (layer, 0, 0)),
            pl.BlockSpec((bm, d), lambda i: (i, 0)),
            pl.BlockSpec((1, d), lambda i: (0, 0)),
        ],
        out_specs=pl.BlockSpec((bm, d), lambda i: (i, 0)),
        out_shape=jax.ShapeDtypeStruct((rows, d), F32),
        compiler_params=_cparams(("parallel",)),
        name="out_proj",
    )(a, b, c, w, x, g.reshape(1, d))


def _ffn_prompt_kernel(x_ref, gpre_ref, wu_ref, wg_ref, wc_ref, bc_ref, wd_ref, gpost_ref,
                       o_ref, tail_ref, h_ref, acc_ref, carry_ref, *, blocks_per_seq):
    i, j = pl.program_id(0), pl.program_id(1)
    bm = x_ref.shape[0]

    @pl.when(j == 0)
    def _():
        h_ref[...] = _rms(x_ref[...], gpre_ref[...]).astype(BF16)
        acc_ref[...] = jnp.zeros_like(acc_ref)

    @pl.when(lax.rem(i, blocks_per_seq) == 0)
    def _():
        carry_ref[j] = jnp.zeros(carry_ref.shape[1:], F32)

    h = h_ref[...]
    u = _dot(h, wu_ref[...])
    g = _dot(h, wg_ref[...])
    prev = carry_ref[j]
    row = lax.broadcasted_iota(jnp.int32, g.shape, 0)
    g1 = jnp.where(row == 0, prev[7:8, :], pltpu.roll(g, 1, axis=0))
    g2 = jnp.where(row == 0, prev[6:7, :], jnp.where(row == 1, prev[7:8, :], pltpu.roll(g, 2, axis=0)))
    gc = wc_ref[0:1, :] * g2 + wc_ref[1:2, :] * g1 + wc_ref[2:3, :] * g + bc_ref[...]
    acc_ref[...] += _dot((_silu(gc) * u).astype(BF16), wd_ref[...])
    tail = g[bm - 8:bm, :]
    carry_ref[j] = tail
    tail_ref[0] = tail

    @pl.when(j == pl.num_programs(1) - 1)
    def _():
        o_ref[...] = x_ref[...] + _rms(acc_ref[...], gpost_ref[...])


def _ffn_sample_kernel(x_ref, gpre_ref, wu_ref, wg_ref, wc_ref, bc_ref, wd_ref, gpost_ref, prev0_ref, prev1_ref,
                       o_ref, tail0_ref, tail1_ref, h_ref, acc_ref, g_scr, u_scr, act_scr, *, nb, ts):
    j = pl.program_id(0)
    n_prev = FFN_K - 1
    prev_refs = (prev0_ref, prev1_ref)
    tail_refs = (tail0_ref, tail1_ref)
    n_chunks = g_scr.shape[0]

    @pl.when(j == 0)
    def _():
        h_ref[...] = _rms(x_ref[...], gpre_ref[...]).astype(BF16)
        acc_ref[...] = jnp.zeros_like(acc_ref)

    h = h_ref[...]
    u = _dot(h, wu_ref[...])
    g = _dot(h, wg_ref[...])
    for c in range(n_chunks):
        lanes = slice(c * HEAD_DIM, (c + 1) * HEAD_DIM)
        u_scr[c] = u[:, lanes]
        g_scr[c] = g[:, lanes]

    for c in range(n_chunks):
        lanes = slice(c * HEAD_DIM, (c + 1) * HEAD_DIM)

        def g_full(t):
            if t < n_prev:
                return prev_refs[t][:, lanes]
            return g_scr[c, pl.ds(t - n_prev, nb, stride=ts), :]

        for t in range(ts):
            gc = bc_ref[:, lanes]
            for k in range(FFN_K):
                gc = gc + wc_ref[k:k + 1, lanes] * g_full(t + k)
            act_scr[c, pl.ds(t, nb, stride=ts), :] = _silu(gc) * u_scr[c, pl.ds(t, nb, stride=ts), :]
        for t in range(n_prev):
            tail_refs[t][:, lanes] = g_full(ts + t)
        acc_ref[...] += _dot(act_scr[c].astype(BF16), wd_ref[lanes, :])

    @pl.when(j == pl.num_programs(0) - 1)
    def _():
        o_ref[...] = x_ref[...] + _rms(acc_ref[...], gpost_ref[...])


def _ffn_prompt(x, gpre, w_up, w_fconv, b_fconv, w_down, layer, gpost, seq, bm, tf):
    rows, d = x.shape
    d_ff = w_down.shape[1]
    nf = d_ff // tf
    kern = functools.partial(_ffn_prompt_kernel, blocks_per_seq=seq // bm)
    return pl.pallas_call(
        kern,
        grid=(rows // bm, nf),
        in_specs=[
            pl.BlockSpec((bm, d), lambda i, j: (i, 0)),
            pl.BlockSpec((1, d), lambda i, j: (0, 0)),
            pl.BlockSpec((None, d, tf), lambda i, j: (layer, 0, j)),
            pl.BlockSpec((None, d, tf), lambda i, j: (layer, 0, nf + j)),
            pl.BlockSpec((FFN_K, tf), lambda i, j: (0, j)),
            pl.BlockSpec((1, tf), lambda i, j: (0, j)),
            pl.BlockSpec((None, tf, d), lambda i, j: (layer, j, 0)),
            pl.BlockSpec((1, d), lambda i, j: (0, 0)),
        ],
        out_specs=[
            pl.BlockSpec((bm, d), lambda i, j: (i, 0)),
            pl.BlockSpec((1, 8, tf), lambda i, j: (i, 0, j)),
        ],
        out_shape=[
            jax.ShapeDtypeStruct((rows, d), F32),
            jax.ShapeDtypeStruct((rows // bm, 8, d_ff), F32),
        ],
        scratch_shapes=[
            pltpu.VMEM((bm, d), BF16),
            pltpu.VMEM((bm, d), F32),
            pltpu.VMEM((nf, 8, tf), F32),
        ],
        compiler_params=_cparams(("arbitrary", "arbitrary")),
        name="ffn_prompt",
    )(x, gpre.reshape(1, d), w_up, w_up, w_fconv, b_fconv.reshape(1, -1), w_down, gpost.reshape(1, d))


def _ffn_sample(x, gpre, w_up, w_fconv, b_fconv, w_down, layer, gpost, prev, nb, ts, tf):
    rows, d = x.shape
    d_ff = w_down.shape[1]
    nf = d_ff // tf
    kern = functools.partial(_ffn_sample_kernel, nb=nb, ts=ts)
    chunk = pltpu.VMEM((tf // HEAD_DIM, rows, HEAD_DIM), F32)
    return pl.pallas_call(
        kern,
        grid=(nf,),
        in_specs=[
            pl.BlockSpec((rows, d), lambda j: (0, 0)),
            pl.BlockSpec((1, d), lambda j: (0, 0)),
            pl.BlockSpec((None, d, tf), lambda j: (layer, 0, j)),
            pl.BlockSpec((None, d, tf), lambda j: (layer, 0, nf + j)),
            pl.BlockSpec((FFN_K, tf), lambda j: (0, j)),
            pl.BlockSpec((1, tf), lambda j: (0, j)),
            pl.BlockSpec((None, tf, d), lambda j: (layer, j, 0)),
            pl.BlockSpec((1, d), lambda j: (0, 0)),
            pl.BlockSpec((nb, tf), lambda j: (0, j)),
            pl.BlockSpec((nb, tf), lambda j: (0, nf + j)),
        ],
        out_specs=[
            pl.BlockSpec((rows, d), lambda j: (0, 0)),
            pl.BlockSpec((nb, tf), lambda j: (0, j)),
            pl.BlockSpec((nb, tf), lambda j: (0, j)),
        ],
        out_shape=[
            jax.ShapeDtypeStruct((rows, d), F32),
            jax.ShapeDtypeStruct((nb, d_ff), F32),
            jax.ShapeDtypeStruct((nb, d_ff), F32),
        ],
        scratch_shapes=[pltpu.VMEM((rows, d), BF16), pltpu.VMEM((rows, d), F32), chunk, chunk, chunk],
        compiler_params=_cparams(("arbitrary",)),
        name="ffn_sample",
    )(x, gpre.reshape(1, d), w_up, w_up, w_fconv, b_fconv.reshape(1, -1), w_down, gpost.reshape(1, d), prev, prev)


def _pick(total, want):
    if total <= want:
        return total
    best = None
    for cand in range(128, want + 1, 128):
        if total % cand == 0:
            best = cand
    assert best is not None, (total, want)
    return best


def kernel(x_prompt, x_sample, cache_k, cache_v, state_conv, state_ret, state_ffn, page_table, g_pre_mix, w_in, w_dw, b_dw, gn_conv_g, gn_conv_b, gn_ret_g, sb_bias, w_out, g_post_mix, g_pre_ffn, w_up, w_fconv, b_fconv, w_down, g_post_ffn):
    bp, tp, d = x_prompt.shape
    bs, ts, _ = x_sample.shape
    depth = w_in.shape[0]
    conv_w = w_dw.shape[2]
    n_ret = gn_ret_g.shape[1] // HEAD_DIM
    n_sb = sb_bias.shape[1]
    d_ff = w_down.shape[1]
    past_len = page_table.shape[1] * cache_k.shape[2]
    assert conv_w % HEAD_DIM == 0 and tp % RET_CHUNK == 0 and tp % SB_BLOCK == 0 and ts < 8

    ret_w, sb_w = n_ret * HEAD_DIM, n_sb * HEAD_DIM
    glu_w = 2 * conv_w
    sq0 = 4 * ret_w
    sk0, sv0 = sq0 + sb_w, sq0 + 2 * sb_w
    assert sq0 % sb_w == 0
    w_glu = w_in[:, :, :glu_w].astype(BF16)
    w_heads = w_in[:, :, glu_w:].astype(BF16)
    w_out_b = w_out.astype(BF16)
    w_up_b = w_up.astype(BF16)
    w_down_b = w_down.astype(BF16)
    page = cache_k.shape[2]
    assert page == SB_BLOCK
    cache_kt = jnp.transpose(cache_k, (0, 1, 3, 2, 4))
    cache_vt = jnp.transpose(cache_v, (0, 1, 3, 2, 4))

    bm_p = _pick(bp * tp, 512)
    bm_seq = _pick(tp, 512)
    bn_glu = _pick(glu_w, 1792)
    bn_heads = _pick(w_heads.shape[2], 1792)
    tf = _pick(d_ff, 512)

    cos_p, sin_p = _rope_tables(jnp.arange(tp))
    cos_s, sin_s = _rope_tables(past_len + jnp.arange(ts))

    xp = x_prompt.reshape(bp * tp, d)
    xs = x_sample.reshape(bs * ts, d)
    outs = [[] for _ in range(10)]
    pages_kv, ret_s = None, None
    for l in range(depth):
        glu = _in_proj(xp, g_pre_mix[l], w_glu, l, bm_p, bn_glu)
        proj = _in_proj(xp, g_pre_mix[l], w_heads, l, bm_p, bn_heads)
        a_out, conv_tail = _conv_prompt(glu, 0, bp, tp, w_dw[l], b_dw[l], gn_conv_g[l], gn_conv_b[l], bm_seq)
        b_out, ret_state = _ret_prompt(proj, bp, tp, n_ret, 0, cos_p, sin_p, gn_ret_g[l])
        c_out, *pages_kv = _sb_prompt(proj, bp, tp, n_sb, sq0 // sb_w, sb_bias[l], l, depth, pages_kv)
        xp = _out_proj(a_out, b_out, c_out, w_out_b, l, xp, g_post_mix[l], _pick(bp * tp, 256))
        xp, ffn_tail = _ffn_prompt(xp, g_pre_ffn[l], w_up_b, w_fconv[l], b_fconv[l], w_down_b, l, g_post_ffn[l],
                                   tp, bm_seq, tf)
        outs[0].append(conv_tail[:, CONV_PAD - (CONV_K - 1):, :])
        outs[1].append(ret_state)
        outs[4].append(ffn_tail.reshape(bp, tp // bm_seq, 8, d_ff)[:, -1, 8 - (FFN_K - 1):, :])

        glu_s = _in_proj(xs, g_pre_mix[l], w_glu, l, bs * ts, bn_glu)
        proj_s3 = _in_proj(xs, g_pre_mix[l], w_heads, l, bs * ts, bn_heads).reshape(bs, ts, -1)
        a_s, a_new = _conv_sample(glu_s, 0, state_conv[l].reshape(bs * (CONV_K - 1), conv_w), bs, ts,
                                  w_dw[l], b_dw[l], gn_conv_g[l], gn_conv_b[l])
        b_s, ret_s = _ret_sample(proj_s3, state_ret, l, ret_s, n_ret, 0, cos_s, sin_s, gn_ret_g[l])
        c_s = _sb_sample(proj_s3, cache_kt, cache_vt, l, page_table, n_sb, sq0, sb_bias[l])
        xs = _out_proj(a_s, b_s.reshape(bs * ts, -1), c_s.reshape(bs * ts, -1), w_out_b, l, xs, g_post_mix[l],
                       bs * ts)
        xs, ffn_t0, ffn_t1 = _ffn_sample(xs, g_pre_ffn[l], w_up_b, w_fconv[l], b_fconv[l], w_down_b, l,
                                         g_post_ffn[l], state_ffn[l].reshape(bs, (FFN_K - 1) * d_ff), bs, ts, tf)
        conv_full = jnp.concatenate([state_conv[l], a_new.reshape(bs, ts, conv_w)], axis=1)
        outs[5].append(conv_full[:, -(CONV_K - 1):, :])
        outs[7].append(proj_s3[:, :, sk0:sk0 + sb_w].reshape(bs, ts, n_sb, HEAD_DIM))
        outs[8].append(proj_s3[:, :, sv0:sv0 + sb_w].reshape(bs, ts, n_sb, HEAD_DIM))
        outs[9].append(jnp.stack([ffn_t0, ffn_t1], axis=1))

    stacked = [jnp.stack(o) if o else None for o in outs]
    stacked[2], stacked[3] = (jnp.transpose(p, (0, 1, 2, 4, 3, 5)) for p in pages_kv)
    stacked[6] = ret_s
    return (xp.reshape(bp, tp, d), xs.reshape(bs, ts, d), *stacked)
```

```python
import functools

import numpy as np
import jax
import jax.numpy as jnp
from jax import lax
from jax.experimental import pallas as pl
from jax.experimental.pallas import tpu as pltpu

F32 = jnp.float32
BF16 = jnp.bfloat16

EPS = 1e-6
HEAD_DIM = 128
CONV_K = 31
CONV_PAD = 32
FFN_K = 3
ROPE_BASE = 10000.0
RET_CHUNK = 128
V7X_VMEM_BYTES = 64 * 1024 * 1024
VMEM_LIMIT = V7X_VMEM_BYTES - 8 * 1024 * 1024


def _cparams(sem):
    return pltpu.CompilerParams(dimension_semantics=sem, vmem_limit_bytes=VMEM_LIMIT)


def _rms(x, g):
    ms = jnp.mean(x * x, axis=-1, keepdims=True)
    return x * lax.rsqrt(ms + EPS) * g


def _group_norm(y):
    mu = jnp.mean(y, axis=-1, keepdims=True)
    d = y - mu
    var = jnp.mean(d * d, axis=-1, keepdims=True)
    return d * lax.rsqrt(var + EPS)


def _silu(x):
    return x * jax.nn.sigmoid(x)


def _softplus(z):
    return jnp.maximum(z, 0.0) + jnp.log(1.0 + jnp.exp(-jnp.abs(z)))


def _dot(a, b):
    return jnp.dot(a, b, preferred_element_type=F32)


def _dot_nt(a, b):
    return lax.dot_general(a, b, (((1,), (1,)), ((), ())), preferred_element_type=F32)


def _dot_tn(a, b):
    return lax.dot_general(a, b, (((0,), (0,)), ((), ())), preferred_element_type=F32)


def _in_proj_kernel(x_ref, g_ref, w_ref, o_ref, h_ref):
    @pl.when(pl.program_id(1) == 0)
    def _():
        h_ref[...] = _rms(x_ref[...], g_ref[...]).astype(BF16)

    o_ref[...] = _dot(h_ref[...], w_ref[...])


def _in_proj(x, g, w, layer, bm, bn):
    rows, d = x.shape
    n = w.shape[2]
    return pl.pallas_call(
        _in_proj_kernel,
        grid=(rows // bm, n // bn),
        in_specs=[
            pl.BlockSpec((bm, d), lambda i, j: (i, 0)),
            pl.BlockSpec((1, d), lambda i, j: (0, 0)),
            pl.BlockSpec((None, d, bn), lambda i, j: (layer, 0, j)),
        ],
        out_specs=pl.BlockSpec((bm, bn), lambda i, j: (i, j)),
        out_shape=jax.ShapeDtypeStruct((rows, n), F32),
        scratch_shapes=[pltpu.VMEM((bm, d), BF16)],
        compiler_params=_cparams(("parallel", "arbitrary")),
        name="in_proj",
    )(x, g.reshape(1, d), w)


def _conv_prompt_kernel(val_ref, gate_ref, w_ref, b_ref, gg_ref, gb_ref, o_ref, st_ref, abuf):
    tb = val_ref.shape[0]
    sub = min(tb, 128)

    @pl.when(pl.program_id(2) == 0)
    def _():
        abuf[0:CONV_PAD, :] = jnp.zeros((CONV_PAD, HEAD_DIM), F32)

    abuf[CONV_PAD:CONV_PAD + tb, :] = val_ref[...] * jax.nn.sigmoid(gate_ref[...])
    lead = CONV_PAD - (CONV_K - 1)
    for r0 in range(0, tb, sub):
        y = jnp.broadcast_to(b_ref[...], (sub, HEAD_DIM))
        for k in range(CONV_K):
            y = y + w_ref[k:k + 1, :] * abuf[r0 + k + lead:r0 + k + lead + sub, :]
        yn = _group_norm(y) * gg_ref[...] + gb_ref[...]
        o_ref[r0:r0 + sub, :] = _silu(yn).astype(o_ref.dtype)
    tail = abuf[tb:tb + CONV_PAD, :]
    st_ref[0] = tail
    abuf[0:CONV_PAD, :] = tail


def _conv_prompt(proj, col0, n_seq, seq, w_dw, b_dw, gn_g, gn_b, tb):
    conv_w = w_dw.shape[1]
    groups = conv_w // HEAD_DIM
    nt = seq // tb
    return pl.pallas_call(
        _conv_prompt_kernel,
        grid=(n_seq, groups, nt),
        in_specs=[
            pl.BlockSpec((tb, HEAD_DIM), lambda n, g, t: (n * nt + t, col0 + g)),
            pl.BlockSpec((tb, HEAD_DIM), lambda n, g, t: (n * nt + t, col0 + groups + g)),
            pl.BlockSpec((CONV_K, HEAD_DIM), lambda n, g, t: (0, g)),
            pl.BlockSpec((1, HEAD_DIM), lambda n, g, t: (0, g)),
            pl.BlockSpec((1, HEAD_DIM), lambda n, g, t: (0, g)),
            pl.BlockSpec((1, HEAD_DIM), lambda n, g, t: (0, g)),
        ],
        out_specs=[
            pl.BlockSpec((tb, HEAD_DIM), lambda n, g, t: (n * nt + t, g)),
            pl.BlockSpec((1, CONV_PAD, HEAD_DIM), lambda n, g, t: (n, 0, g)),
        ],
        out_shape=[
            jax.ShapeDtypeStruct((n_seq * seq, conv_w), BF16),
            jax.ShapeDtypeStruct((n_seq, CONV_PAD, conv_w), F32),
        ],
        scratch_shapes=[pltpu.VMEM((CONV_PAD + tb, HEAD_DIM), F32)],
        compiler_params=_cparams(("parallel", "parallel", "arbitrary")),
        name="conv_prompt",
    )(proj, proj, w_dw, b_dw.reshape(1, -1), gn_g.reshape(1, -1), gn_b.reshape(1, -1))


def _conv_sample_kernel(val_ref, gate_ref, st_ref, w_ref, b_ref, gg_ref, gb_ref, o_ref, a_ref, *, nb, ts, n_prev):
    a_ref[...] = val_ref[...] * jax.nn.sigmoid(gate_ref[...])

    def a_full(j):
        if j < n_prev:
            return st_ref[pl.ds(j, nb, stride=n_prev), :]
        return a_ref[pl.ds(j - n_prev, nb, stride=ts), :]

    for t in range(ts):
        y = jnp.broadcast_to(b_ref[...], (nb, HEAD_DIM))
        for k in range(CONV_K):
            y = y + w_ref[k:k + 1, :] * a_full(t + k)
        yn = _group_norm(y) * gg_ref[...] + gb_ref[...]
        o_ref[pl.ds(t, nb, stride=ts), :] = _silu(yn)


def _conv_sample(proj, col0, state2d, nb, ts, w_dw, b_dw, gn_g, gn_b):
    conv_w = w_dw.shape[1]
    groups = conv_w // HEAD_DIM
    n_prev = CONV_K - 1
    rows = nb * ts
    kern = functools.partial(_conv_sample_kernel, nb=nb, ts=ts, n_prev=n_prev)
    return pl.pallas_call(
        kern,
        grid=(groups,),
        in_specs=[
            pl.BlockSpec((rows, HEAD_DIM), lambda g: (0, col0 + g)),
            pl.BlockSpec((rows, HEAD_DIM), lambda g: (0, col0 + groups + g)),
            pl.BlockSpec((nb * n_prev, HEAD_DIM), lambda g: (0, g)),
            pl.BlockSpec((CONV_K, HEAD_DIM), lambda g: (0, g)),
            pl.BlockSpec((1, HEAD_DIM), lambda g: (0, g)),
            pl.BlockSpec((1, HEAD_DIM), lambda g: (0, g)),
            pl.BlockSpec((1, HEAD_DIM), lambda g: (0, g)),
        ],
        out_specs=[
            pl.BlockSpec((rows, HEAD_DIM), lambda g: (0, g)),
            pl.BlockSpec((rows, HEAD_DIM), lambda g: (0, g)),
        ],
        out_shape=[
            jax.ShapeDtypeStruct((rows, conv_w), F32),
            jax.ShapeDtypeStruct((rows, conv_w), F32),
        ],
        compiler_params=_cparams(("parallel",)),
        name="conv_sample",
    )(proj, proj, state2d, w_dw, b_dw.reshape(1, -1), gn_g.reshape(1, -1), gn_b.reshape(1, -1))


def _rope(x, cos2, sin_signed):
    return x * cos2 + pltpu.roll(x, HEAD_DIM // 2, axis=1) * sin_signed


def _ret_prompt_kernel(q_ref, k_ref, v_ref, g_ref, cos_ref, sin_ref, dec_ref, xi_ref, zeta_ref, gl_ref, gn_ref,
                       o_ref, so_ref, s_ref):
    @pl.when(pl.program_id(1) == 0)
    def _():
        s_ref[...] = jnp.zeros_like(s_ref)

    cos2, sin_s = cos_ref[...], sin_ref[...]
    heads = range(s_ref.shape[0])
    lanes = [slice(h * HEAD_DIM, (h + 1) * HEAD_DIM) for h in heads]
    qs = [_rope(q_ref[:, ln], cos2, sin_s).astype(BF16) for ln in lanes]
    ks = [_rope(k_ref[:, ln], cos2, sin_s) * (HEAD_DIM ** -0.5) for ln in lanes]
    vs = [v_ref[:, ln].astype(BF16) for ln in lanes]
    scores = [_dot_nt(qs[h], ks[h].astype(BF16)) * dec_ref[h] for h in heads]
    cross = [_dot(qs[h], s_ref[h].astype(BF16)) * xi_ref[h] for h in heads]
    kv = [_dot_tn((ks[h] * zeta_ref[h]).astype(BF16), vs[h]) for h in heads]
    inner = [_dot(scores[h].astype(BF16), vs[h]) for h in heads]
    for h in heads:
        s_new = gl_ref[h] * s_ref[h] + kv[h]
        s_ref[h] = s_new
        so_ref[0, h] = s_new
        r = inner[h] + cross[h]
        o_ref[:, lanes[h]] = (_group_norm(r) * gn_ref[:, lanes[h]] * _silu(g_ref[:, lanes[h]])).astype(o_ref.dtype)


def _ret_consts(length, n_heads):
    log_gamma = np.log1p(-np.exp2(-5.0 - np.arange(n_heads, dtype=np.float64)))
    idx = np.arange(RET_CHUNK, dtype=np.float64)
    diff = idx[:, None] - idx[None, :]
    live = (diff >= 0) & (idx[:, None] < length) & (idx[None, :] < length)
    dec = np.where(live[None], np.exp(np.maximum(diff, 0.0)[None] * log_gamma[:, None, None]), 0.0)
    xi = np.exp((idx[None, :] + 1.0) * log_gamma[:, None])
    zeta = np.where(idx[None, :] < length, np.exp((length - 1.0 - idx)[None, :] * log_gamma[:, None]), 0.0)
    gl = np.exp(length * log_gamma)
    bc = lambda a: jnp.asarray(np.broadcast_to(a[:, :, None], (n_heads, RET_CHUNK, HEAD_DIM)), F32)
    return (jnp.asarray(dec, F32), bc(xi), bc(zeta),
            jnp.asarray(np.broadcast_to(gl[:, None, None], (n_heads, 1, HEAD_DIM)), F32))


def _rope_tables(pos):
    half = HEAD_DIM // 2
    inv = ROPE_BASE ** (-jnp.arange(half, dtype=F32) / half)
    ang = pos.astype(F32)[:, None] * inv[None, :]
    cos, sin = jnp.cos(ang), jnp.sin(ang)
    return jnp.concatenate([cos, cos], axis=1), jnp.concatenate([-sin, sin], axis=1)


def _ret_prompt(proj, n_seq, seq, n_heads, seg0, cos2, sin_s, gn_g):
    nc = seq // RET_CHUNK
    width = n_heads * HEAD_DIM
    dec, xi, zeta, gl = _ret_consts(RET_CHUNK, n_heads)
    blk = lambda off: pl.BlockSpec((RET_CHUNK, width), lambda n, c: (n * nc + c, seg0 + off))
    full = lambda a: pl.BlockSpec(a.shape, lambda n, c: (0,) * a.ndim)
    return pl.pallas_call(
        _ret_prompt_kernel,
        grid=(n_seq, nc),
        in_specs=[
            blk(0), blk(1), blk(2), blk(3),
            pl.BlockSpec((RET_CHUNK, HEAD_DIM), lambda n, c: (c, 0)),
            pl.BlockSpec((RET_CHUNK, HEAD_DIM), lambda n, c: (c, 0)),
            full(dec), full(xi), full(zeta), full(gl),
            pl.BlockSpec((1, width), lambda n, c: (0, 0)),
        ],
        out_specs=[
            pl.BlockSpec((RET_CHUNK, width), lambda n, c: (n * nc + c, 0)),
            pl.BlockSpec((1, n_heads, HEAD_DIM, HEAD_DIM), lambda n, c: (n, 0, 0, 0)),
        ],
        out_shape=[
            jax.ShapeDtypeStruct((n_seq * seq, width), BF16),
            jax.ShapeDtypeStruct((n_seq, n_heads, HEAD_DIM, HEAD_DIM), F32),
        ],
        scratch_shapes=[pltpu.VMEM((n_heads, HEAD_DIM, HEAD_DIM), F32)],
        compiler_params=_cparams(("parallel", "arbitrary")),
        name="ret_prompt",
    )(proj, proj, proj, proj, cos2, sin_s, dec, xi, zeta, gl, gn_g.reshape(1, -1))


def _ret_sample_kernel(p_ref, s_ref, cos_ref, sin_ref, dec_ref, xi_ref, zeta_ref, gl_ref, gn_ref, *refs,
                       n_heads, col0, ts):
    o_ref, so_ref, qp, kp, kzp, vp = refs[-6:]
    cos2, sin_s = cos_ref[...], sin_ref[...]
    heads = range(n_heads)
    cols = lambda off, h: slice((col0 + off * n_heads + h) * HEAD_DIM, (col0 + off * n_heads + h + 1) * HEAD_DIM)
    qp[...] = jnp.zeros_like(qp)
    kp[...] = jnp.zeros_like(kp)
    kzp[...] = jnp.zeros_like(kzp)
    vp[...] = jnp.zeros_like(vp)
    for h in heads:
        k = _rope(p_ref[0, :, cols(1, h)], cos2, sin_s) * (HEAD_DIM ** -0.5)
        qp[h, 0:ts, :] = _rope(p_ref[0, :, cols(0, h)], cos2, sin_s)
        kp[h, 0:ts, :] = k
        kzp[h, 0:ts, :] = k * zeta_ref[h, 0:ts, :]
        vp[h, 0:ts, :] = p_ref[0, :, cols(2, h)]
    qs = [qp[h].astype(BF16) for h in heads]
    vs = [vp[h].astype(BF16) for h in heads]
    scores = [_dot_nt(qs[h], kp[h].astype(BF16)) * dec_ref[h, 0:8, :] for h in heads]
    cross = [_dot(qs[h], s_ref[0, h].astype(BF16)) * xi_ref[h, 0:8, :] for h in heads]
    kv = [_dot_tn(kzp[h].astype(BF16), vs[h]) for h in heads]
    inner = [_dot(scores[h].astype(BF16), vs[h]) for h in heads]
    for h in heads:
        so_ref[0, h] = gl_ref[h] * s_ref[0, h] + kv[h]
        r = (inner[h] + cross[h])[0:ts]
        gn = gn_ref[:, h * HEAD_DIM:(h + 1) * HEAD_DIM]
        o_ref[0, :, h * HEAD_DIM:(h + 1) * HEAD_DIM] = _group_norm(r) * gn * _silu(p_ref[0, :, cols(3, h)])


def _ret_sample(proj3, state, layer, new_state, n_heads, col0, cos2, sin_s, gn_g):
    nb, ts, n_cols = proj3.shape
    dec, xi, zeta, gl = _ret_consts(ts, n_heads)
    kern = functools.partial(_ret_sample_kernel, n_heads=n_heads, col0=col0, ts=ts)
    full = lambda a: pl.BlockSpec(a.shape, lambda b: (0,) * a.ndim)
    state_spec = pl.BlockSpec((None, 1, n_heads, HEAD_DIM, HEAD_DIM), lambda b: (layer, b, 0, 0, 0))
    earlier = [] if new_state is None else [new_state]
    return pl.pallas_call(
        kern,
        grid=(nb,),
        in_specs=[
            pl.BlockSpec((1, ts, n_cols), lambda b: (b, 0, 0)),
            state_spec,
            full(cos2), full(sin_s), full(dec), full(xi), full(zeta), full(gl),
            pl.BlockSpec((1, n_heads * HEAD_DIM), lambda b: (0, 0)),
        ] + [pl.BlockSpec(memory_space=pl.ANY)] * len(earlier),
        out_specs=[pl.BlockSpec((1, ts, n_heads * HEAD_DIM), lambda b: (b, 0, 0)), state_spec],
        out_shape=[
            jax.ShapeDtypeStruct((nb, ts, n_heads * HEAD_DIM), F32),
            jax.ShapeDtypeStruct(state.shape, F32),
        ],
        input_output_aliases={9 + a: 1 + a for a in range(len(earlier))},
        scratch_shapes=[
            pltpu.VMEM((n_heads, 8, HEAD_DIM), F32),
            pltpu.VMEM((n_heads, RET_CHUNK, HEAD_DIM), F32),
            pltpu.VMEM((n_heads, RET_CHUNK, HEAD_DIM), F32),
            pltpu.VMEM((n_heads, RET_CHUNK, HEAD_DIM), F32),
        ],
        compiler_params=_cparams(("parallel",)),
        name="ret_sample",
    )(proj3, state, cos2, sin_s, dec, xi, zeta, gl, gn_g.reshape(1, -1), *earlier)


SB_BLOCK = 128


def _suffix_matrix():
    j = np.arange(SB_BLOCK)
    strict = -(j[:, None] > j[None, :]).astype(np.float32)
    half = np.concatenate([strict, -np.ones((SB_BLOCK, SB_BLOCK), np.float32)], axis=1)
    return jnp.asarray(np.concatenate([half, half], axis=0), BF16)


def _suffix_sums(sp, valid, suffix):
    drop = sp if valid is None else jnp.where(valid, sp, 0.0)
    hi = drop.astype(BF16)
    lo = (drop - hi.astype(F32)).astype(BF16)
    return _dot(jnp.concatenate([hi, lo], axis=1), suffix)


def _sb_prompt_kernel(q_ref, k_ref, v_ref, bias_ref, suf_ref, *refs, n_heads):
    o_ref, pk_ref, pv_ref, qb_ref, kb_ref, vb_ref, acc_ref, c_ref = refs[-8:]
    i = pl.program_id(1)

    @pl.when(i == 0)
    def _():
        kb_ref[...] = k_ref[...].astype(BF16)
        vb_ref[...] = v_ref[...].astype(BF16)

    qb_ref[...] = q_ref[...].astype(BF16)
    acc_ref[...] = jnp.zeros_like(acc_ref)
    c_ref[...] = jnp.zeros_like(c_ref)

    lanes = [slice(h * HEAD_DIM, (h + 1) * HEAD_DIM) for h in range(n_heads)]
    own = pl.ds(pl.multiple_of(i * SB_BLOCK, SB_BLOCK), SB_BLOCK)
    for h, ln in enumerate(lanes):
        pk_ref[h] = k_ref[own, ln]
        pv_ref[h] = v_ref[own, ln]

    def sweep(blocks, valid):
        offs = [j * SB_BLOCK if isinstance(j, int) else pl.multiple_of(j * SB_BLOCK, SB_BLOCK) for j in blocks]
        zs = [[_dot_nt(qb_ref[:, ln], kb_ref[pl.ds(off, SB_BLOCK), ln]) * (HEAD_DIM ** -0.5) + bias_ref[:, ln]
               for ln in lanes] for off in offs]
        sps = [[_softplus(z) for z in row_z] for row_z in zs]
        cums = [[_suffix_sums(sp, valid, suf_ref[...]) for sp in row_sp] for row_sp in sps]
        for h, ln in enumerate(lanes):
            c = c_ref[:, ln]
            pv = None
            for b, off in enumerate(offs):
                w = jnp.exp(zs[b][h] - sps[b][h] + cums[b][h][:, :SB_BLOCK] + c)
                if valid is not None:
                    w = jnp.where(valid, w, 0.0)
                d = _dot(w.astype(BF16), vb_ref[pl.ds(off, SB_BLOCK), ln])
                pv = d if pv is None else pv + d
                c = c + cums[b][h][:, SB_BLOCK:]
            acc_ref[:, ln] += pv
            c_ref[:, ln] = c

    row = lax.broadcasted_iota(jnp.int32, (SB_BLOCK, SB_BLOCK), 0)
    col = lax.broadcasted_iota(jnp.int32, (SB_BLOCK, SB_BLOCK), 1)
    sweep([i], col < row)

    def body(t, carry):
        j = i - 1 - 2 * t
        sweep([j, j - 1], None)
        return carry

    lax.fori_loop(0, lax.shift_right_logical(i, 1), body, 0)

    @pl.when(jnp.bitwise_and(i, 1) == 1)
    def _():
        sweep([0], None)

    o_ref[...] = acc_ref[...].astype(o_ref.dtype)


def _sb_prompt(proj, n_seq, seq, n_heads, seg0, sb_bias, layer, depth, pages_kv):
    nq = seq // SB_BLOCK
    width = n_heads * HEAD_DIM
    suffix = _suffix_matrix()
    bias_row = jnp.repeat(sb_bias.astype(F32), HEAD_DIM).reshape(1, width)
    kern = functools.partial(_sb_prompt_kernel, n_heads=n_heads)
    pages_shape = jax.ShapeDtypeStruct((depth, n_seq, nq, n_heads, SB_BLOCK, HEAD_DIM), F32)
    pages_spec = pl.BlockSpec((None, None, None, n_heads, SB_BLOCK, HEAD_DIM), lambda n, i: (layer, n, i, 0, 0, 0))
    earlier = [] if pages_kv is None else list(pages_kv)
    return pl.pallas_call(
        kern,
        grid=(n_seq, nq),
        in_specs=[
            pl.BlockSpec((SB_BLOCK, width), lambda n, i: (n * nq + i, seg0)),
            pl.BlockSpec((seq, width), lambda n, i: (n, seg0 + 1)),
            pl.BlockSpec((seq, width), lambda n, i: (n, seg0 + 2)),
            pl.BlockSpec((1, width), lambda n, i: (0, 0)),
            pl.BlockSpec(suffix.shape, lambda n, i: (0, 0)),
        ] + [pl.BlockSpec(memory_space=pl.ANY)] * len(earlier),
        out_specs=[pl.BlockSpec((SB_BLOCK, width), lambda n, i: (n * nq + i, 0)), pages_spec, pages_spec],
        out_shape=[jax.ShapeDtypeStruct((n_seq * seq, width), BF16), pages_shape, pages_shape],
        input_output_aliases={5 + a: 1 + a for a in range(len(earlier))},
        scratch_shapes=[
            pltpu.VMEM((SB_BLOCK, width), BF16),
            pltpu.VMEM((seq, width), BF16),
            pltpu.VMEM((seq, width), BF16),
            pltpu.VMEM((SB_BLOCK, width), F32),
            pltpu.VMEM((SB_BLOCK, width), F32),
        ],
        compiler_params=_cparams(("parallel", "arbitrary")),
        name="sb_prompt",
    )(proj, proj, proj, bias_row, suffix, *earlier)


Q_TILE = 8


def _sb_sample_kernel(pt_ref, p_ref, *refs, n_heads, ts, n_pages, sq0):
    k_refs, v_refs = refs[:n_pages], refs[n_pages:2 * n_pages]
    bias_ref, suf_ref, o_ref, q8, knew, vnew, z_scr, w_scr = refs[2 * n_pages:]
    width = n_heads * HEAD_DIM
    grp = n_heads * Q_TILE

    q8[...] = jnp.zeros_like(q8)
    knew[...] = jnp.zeros_like(knew)
    vnew[...] = jnp.zeros_like(vnew)
    for h in range(n_heads):
        c0 = sq0 + h * HEAD_DIM
        q8[h, 0:ts, :] = p_ref[0, :, c0:c0 + HEAD_DIM]
        knew[h, 0:ts, :] = p_ref[0, :, c0 + width:c0 + width + HEAD_DIM]
        vnew[h, 0:ts, :] = p_ref[0, :, c0 + 2 * width:c0 + 2 * width + HEAD_DIM]

    def tile_of(page_refs, own, p, h):
        return (page_refs[p][h] if p < n_pages else own[h]).astype(BF16)

    for p in range(n_pages + 1):
        for h in range(n_heads):
            r0 = (p * n_heads + h) * Q_TILE
            z = _dot_nt(q8[h].astype(BF16), tile_of(k_refs, knew, p, h))
            z_scr[r0:r0 + Q_TILE, :] = z * (HEAD_DIM ** -0.5) + bias_ref[h]

    z = z_scr[...]
    row = lax.broadcasted_iota(jnp.int32, z.shape, 0)
    col = lax.broadcasted_iota(jnp.int32, z.shape, 1)
    valid = (row < n_pages * grp) | (col < jnp.bitwise_and(row, Q_TILE - 1))
    sp = _softplus(z)
    cum = _suffix_sums(sp, valid, suf_ref[...])
    tot = cum[:, SB_BLOCK:]
    later = [None] * (n_pages + 1)
    run = jnp.zeros((grp, SB_BLOCK), F32)
    for p in reversed(range(n_pages + 1)):
        later[p] = run
        run = run + tot[p * grp:(p + 1) * grp]
    c = jnp.concatenate(later, axis=0)
    w_scr[...] = jnp.where(valid, jnp.exp(z - sp + cum[:, :SB_BLOCK] + c), 0.0)

    for h in range(n_heads):
        acc = jnp.zeros((Q_TILE, HEAD_DIM), F32)
        for p in range(n_pages + 1):
            r0 = (p * n_heads + h) * Q_TILE
            acc = acc + _dot(w_scr[r0:r0 + Q_TILE, :].astype(BF16), tile_of(v_refs, vnew, p, h))
        o_ref[0, :, h * HEAD_DIM:(h + 1) * HEAD_DIM] = acc[0:ts]


def _sb_sample(proj3, cache_k, cache_v, layer, page_table, n_heads, sq0, sb_bias):
    nb, ts, n_cols = proj3.shape
    n_pages = page_table.shape[1]
    page = cache_k.shape[3]
    assert page == SB_BLOCK and ts < Q_TILE
    width = n_heads * HEAD_DIM
    rows = (n_pages + 1) * n_heads * Q_TILE
    suffix = _suffix_matrix()
    bias = jnp.broadcast_to(sb_bias.astype(F32)[:, None, None], (n_heads, 1, HEAD_DIM))
    kern = functools.partial(_sb_sample_kernel, n_heads=n_heads, ts=ts, n_pages=n_pages, sq0=sq0)
    page_spec = lambda p: pl.BlockSpec((None, None, n_heads, page, HEAD_DIM),
                                       lambda b, pt: (layer, pt[b * n_pages + p], 0, 0, 0))
    grid_spec = pltpu.PrefetchScalarGridSpec(
        num_scalar_prefetch=1,
        grid=(nb,),
        in_specs=(
            [pl.BlockSpec((1, ts, n_cols), lambda b, pt: (b, 0, 0))]
            + [page_spec(p) for p in range(n_pages)] * 2
            + [pl.BlockSpec(bias.shape, lambda b, pt: (0, 0, 0)),
               pl.BlockSpec(suffix.shape, lambda b, pt: (0, 0))]
        ),
        out_specs=pl.BlockSpec((1, ts, width), lambda b, pt: (b, 0, 0)),
        scratch_shapes=[
            pltpu.VMEM((n_heads, Q_TILE, HEAD_DIM), F32),
            pltpu.VMEM((n_heads, page, HEAD_DIM), F32),
            pltpu.VMEM((n_heads, page, HEAD_DIM), F32),
            pltpu.VMEM((rows, SB_BLOCK), F32),
            pltpu.VMEM((rows, SB_BLOCK), F32),
        ],
    )
    return pl.pallas_call(
        kern,
        grid_spec=grid_spec,
        out_shape=jax.ShapeDtypeStruct((nb, ts, width), F32),
        compiler_params=_cparams(("parallel",)),
        name="sb_sample",
    )(page_table.reshape(-1), proj3, *([cache_k] * n_pages), *([cache_v] * n_pages), bias, suffix)


def _out_proj_kernel(a_ref, b_ref, c_ref, w_ref, x_ref, g_ref, o_ref, *, wa, wb):
    mix = (_dot(a_ref[...].astype(BF16), w_ref[0:wa, :])
           + _dot(b_ref[...].astype(BF16), w_ref[wa:wa + wb, :])
           + _dot(c_ref[...].astype(BF16), w_ref[wa + wb:, :]))
    o_ref[...] = x_ref[...] + _rms(mix, g_ref[...])


def _out_proj(a, b, c, w, layer, x, g, bm):
    rows, d = x.shape
    wa, wb, wc = a.shape[1], b.shape[1], c.shape[1]
    kern = functools.partial(_out_proj_kernel, wa=wa, wb=wb)
    return pl.pallas_call(
        kern,
        grid=(rows // bm,),
        in_specs=[
            pl.BlockSpec((bm, wa), lambda i: (i, 0)),
            pl.BlockSpec((bm, wb), lambda i: (i, 0)),
            pl.BlockSpec((bm, wc), lambda i: (i, 0)),
            pl.BlockSpec((None,) + w.shape[1:], lambda i: (layer, 0, 0)),
            pl.BlockSpec((bm, d), lambda i: (i, 0)),
            pl.BlockSpec((1, d), lambda i: (0, 0)),
        ],
        out_specs=pl.BlockSpec((bm, d), lambda i: (i, 0)),
        out_shape=jax.ShapeDtypeStruct((rows, d), F32),
        compiler_params=_cparams(("parallel",)),
        name="out_proj",
    )(a, b, c, w, x, g.reshape(1, d))


def _ffn_prompt_kernel(x_ref, gpre_ref, wu_ref, wg_ref, wc_ref, bc_ref, wd_ref, gpost_ref,
                       o_ref, tail_ref, h_ref, acc_ref, carry_ref, *, blocks_per_seq):
    i, j = pl.program_id(0), pl.program_id(1)
    bm = x_ref.shape[0]

    @pl.when(j == 0)
    def _():
        h_ref[...] = _rms(x_ref[...], gpre_ref[...]).astype(BF16)
        acc_ref[...] = jnp.zeros_like(acc_ref)

    @pl.when(lax.rem(i, blocks_per_seq) == 0)
    def _():
        carry_ref[j] = jnp.zeros(carry_ref.shape[1:], F32)

    h = h_ref[...]
    g = _dot(h, wg_ref[...])
    prev = carry_ref[j]
    row = lax.broadcasted_iota(jnp.int32, g.shape, 0)
    g1 = jnp.where(row == 0, prev[7:8, :], pltpu.roll(g, 1, axis=0))
    g2 = jnp.where(row == 0, prev[6:7, :], jnp.where(row == 1, prev[7:8, :], pltpu.roll(g, 2, axis=0)))
    gate = _silu(wc_ref[0:1, :] * g2 + wc_ref[1:2, :] * g1 + wc_ref[2:3, :] * g + bc_ref[...])
    u = _dot(h, wu_ref[...])
    acc_ref[...] += _dot((gate * u).astype(BF16), wd_ref[...])
    tail = g[bm - 8:bm, :]
    carry_ref[j] = tail
    tail_ref[0] = tail

    @pl.when(j == pl.num_programs(1) - 1)
    def _():
        o_ref[...] = x_ref[...] + _rms(acc_ref[...], gpost_ref[...])


def _ffn_sample_kernel(x_ref, gpre_ref, wu_ref, wg_ref, wc_ref, bc_ref, wd_ref, gpost_ref, prev0_ref, prev1_ref,
                       o_ref, tail0_ref, tail1_ref, h_ref, acc_ref, g_scr, u_scr, act_scr, *, nb, ts):
    j = pl.program_id(0)
    n_prev = FFN_K - 1
    prev_refs = (prev0_ref, prev1_ref)
    tail_refs = (tail0_ref, tail1_ref)
    n_chunks = g_scr.shape[0]

    @pl.when(j == 0)
    def _():
        h_ref[...] = _rms(x_ref[...], gpre_ref[...]).astype(BF16)
        acc_ref[...] = jnp.zeros_like(acc_ref)

    h = h_ref[...]
    u = _dot(h, wu_ref[...])
    g = _dot(h, wg_ref[...])
    for c in range(n_chunks):
        lanes = slice(c * HEAD_DIM, (c + 1) * HEAD_DIM)
        u_scr[c] = u[:, lanes]
        g_scr[c] = g[:, lanes]

    for c in range(n_chunks):
        lanes = slice(c * HEAD_DIM, (c + 1) * HEAD_DIM)

        def g_full(t):
            if t < n_prev:
                return prev_refs[t][:, lanes]
            return g_scr[c, pl.ds(t - n_prev, nb, stride=ts), :]

        for t in range(ts):
            gc = bc_ref[:, lanes]
            for k in range(FFN_K):
                gc = gc + wc_ref[k:k + 1, lanes] * g_full(t + k)
            act_scr[c, pl.ds(t, nb, stride=ts), :] = _silu(gc) * u_scr[c, pl.ds(t, nb, stride=ts), :]
        for t in range(n_prev):
            tail_refs[t][:, lanes] = g_full(ts + t)
        acc_ref[...] += _dot(act_scr[c].astype(BF16), wd_ref[lanes, :])

    @pl.when(j == pl.num_programs(0) - 1)
    def _():
        o_ref[...] = x_ref[...] + _rms(acc_ref[...], gpost_ref[...])


def _ffn_prompt(x, gpre, w_up, w_fconv, b_fconv, w_down, layer, gpost, seq, bm, tf):
    rows, d = x.shape
    d_ff = w_down.shape[1]
    nf = d_ff // tf
    kern = functools.partial(_ffn_prompt_kernel, blocks_per_seq=seq // bm)
    return pl.pallas_call(
        kern,
        grid=(rows // bm, nf),
        in_specs=[
            pl.BlockSpec((bm, d), lambda i, j: (i, 0)),
            pl.BlockSpec((1, d), lambda i, j: (0, 0)),
            pl.BlockSpec((None, d, tf), lambda i, j: (layer, 0, j)),
            pl.BlockSpec((None, d, tf), lambda i, j: (layer, 0, nf + j)),
            pl.BlockSpec((FFN_K, tf), lambda i, j: (0, j)),
            pl.BlockSpec((1, tf), lambda i, j: (0, j)),
            pl.BlockSpec((None, tf, d), lambda i, j: (layer, j, 0)),
            pl.BlockSpec((1, d), lambda i, j: (0, 0)),
        ],
        out_specs=[
            pl.BlockSpec((bm, d), lambda i, j: (i, 0)),
            pl.BlockSpec((1, 8, tf), lambda i, j: (i, 0, j)),
        ],
        out_shape=[
            jax.ShapeDtypeStruct((rows, d), F32),
            jax.ShapeDtypeStruct((rows // bm, 8, d_ff), F32),
        ],
        scratch_shapes=[
            pltpu.VMEM((bm, d), BF16),
            pltpu.VMEM((bm, d), F32),
            pltpu.VMEM((nf, 8, tf), F32),
        ],
        compiler_params=_cparams(("arbitrary", "arbitrary")),
        name="ffn_prompt",
    )(x, gpre.reshape(1, d), w_up, w_up, w_fconv, b_fconv.reshape(1, -1), w_down, gpost.reshape(1, d))


def _ffn_sample(x, gpre, w_up, w_fconv, b_fconv, w_down, layer, gpost, prev, nb, ts, tf):
    rows, d = x.shape
    d_ff = w_down.shape[1]
    nf = d_ff // tf
    kern = functools.partial(_ffn_sample_kernel, nb=nb, ts=ts)
    chunk = pltpu.VMEM((tf // HEAD_DIM, rows, HEAD_DIM), F32)
    return pl.pallas_call(
        kern,
        grid=(nf,),
        in_specs=[
            pl.BlockSpec((rows, d), lambda j: (0, 0)),
            pl.BlockSpec((1, d), lambda j: (0, 0)),
            pl.BlockSpec((None, d, tf), lambda j: (layer, 0, j)),
            pl.BlockSpec((None, d, tf), lambda j: (layer, 0, nf + j)),
            pl.BlockSpec((FFN_K, tf), lambda j: (0, j)),
            pl.BlockSpec((1, tf), lambda j: (0, j)),
            pl.BlockSpec((None, tf, d), lambda j: (layer, j, 0)),
            pl.BlockSpec((1, d), lambda j: (0, 0)),
            pl.BlockSpec((nb, tf), lambda j: (0, j)),
            pl.BlockSpec((nb, tf), lambda j: (0, nf + j)),
        ],
        out_specs=[
            pl.BlockSpec((rows, d), lambda j: (0, 0)),
            pl.BlockSpec((nb, tf), lambda j: (0, j)),
            pl.BlockSpec((nb, tf), lambda j: (0, j)),
        ],
        out_shape=[
            jax.ShapeDtypeStruct((rows, d), F32),
            jax.ShapeDtypeStruct((nb, d_ff), F32),
            jax.ShapeDtypeStruct((nb, d_ff), F32),
        ],
        scratch_shapes=[pltpu.VMEM((rows, d), BF16), pltpu.VMEM((rows, d), F32), chunk, chunk, chunk],
        compiler_params=_cparams(("arbitrary",)),
        name="ffn_sample",
    )(x, gpre.reshape(1, d), w_up, w_up, w_fconv, b_fconv.reshape(1, -1), w_down, gpost.reshape(1, d), prev, prev)


def _pick(total, want):
    if total <= want:
        return total
    best = None
    for cand in range(128, want + 1, 128):
        if total % cand == 0:
            best = cand
    assert best is not None, (total, want)
    return best


def kernel(x_prompt, x_sample, cache_k, cache_v, state_conv, state_ret, state_ffn, page_table, g_pre_mix, w_in, w_dw, b_dw, gn_conv_g, gn_conv_b, gn_ret_g, sb_bias, w_out, g_post_mix, g_pre_ffn, w_up, w_fconv, b_fconv, w_down, g_post_ffn):
    bp, tp, d = x_prompt.shape
    bs, ts, _ = x_sample.shape
    depth = w_in.shape[0]
    conv_w = w_dw.shape[2]
    n_ret = gn_ret_g.shape[1] // HEAD_DIM
    n_sb = sb_bias.shape[1]
    d_ff = w_down.shape[1]
    past_len = page_table.shape[1] * cache_k.shape[2]
    assert conv_w % HEAD_DIM == 0 and tp % RET_CHUNK == 0 and tp % SB_BLOCK == 0 and ts < 8

    ret_w, sb_w = n_ret * HEAD_DIM, n_sb * HEAD_DIM
    glu_w = 2 * conv_w
    sq0 = 4 * ret_w
    sk0, sv0 = sq0 + sb_w, sq0 + 2 * sb_w
    assert sq0 % sb_w == 0
    w_glu = w_in[:, :, :glu_w].astype(BF16)
    w_heads = w_in[:, :, glu_w:].astype(BF16)
    w_out_b = w_out.astype(BF16)
    w_up_b = w_up.astype(BF16)
    w_down_b = w_down.astype(BF16)
    page = cache_k.shape[2]
    assert page == SB_BLOCK
    cache_kt = jnp.transpose(cache_k, (0, 1, 3, 2, 4))
    cache_vt = jnp.transpose(cache_v, (0, 1, 3, 2, 4))

    bm_p = _pick(bp * tp, 1024)
    bm_out = _pick(bp * tp, 512)
    bm_seq = _pick(tp, 512)
    bn_glu = _pick(glu_w, 1792)
    bn_heads = _pick(w_heads.shape[2], 1792)
    tf = _pick(d_ff, 512)

    cos_p, sin_p = _rope_tables(jnp.arange(tp))
    cos_s, sin_s = _rope_tables(past_len + jnp.arange(ts))

    xp = x_prompt.reshape(bp * tp, d)
    xs = x_sample.reshape(bs * ts, d)
    outs = [[] for _ in range(10)]
    pages_kv, ret_s = None, None
    for l in range(depth):
        glu = _in_proj(xp, g_pre_mix[l], w_glu, l, bm_p, bn_glu)
        proj = _in_proj(xp, g_pre_mix[l], w_heads, l, bm_p, bn_heads)
        a_out, conv_tail = _conv_prompt(glu, 0, bp, tp, w_dw[l], b_dw[l], gn_conv_g[l], gn_conv_b[l], bm_seq)
        b_out, ret_state = _ret_prompt(proj, bp, tp, n_ret, 0, cos_p, sin_p, gn_ret_g[l])
        c_out, *pages_kv = _sb_prompt(proj, bp, tp, n_sb, sq0 // sb_w, sb_bias[l], l, depth, pages_kv)
        xp = _out_proj(a_out, b_out, c_out, w_out_b, l, xp, g_post_mix[l], bm_out)
        xp, ffn_tail = _ffn_prompt(xp, g_pre_ffn[l], w_up_b, w_fconv[l], b_fconv[l], w_down_b, l, g_post_ffn[l],
                                   tp, bm_seq, tf)
        outs[0].append(conv_tail[:, CONV_PAD - (CONV_K - 1):, :])
        outs[1].append(ret_state)
        outs[4].append(ffn_tail.reshape(bp, tp // bm_seq, 8, d_ff)[:, -1, 8 - (FFN_K - 1):, :])

        glu_s = _in_proj(xs, g_pre_mix[l], w_glu, l, bs * ts, bn_glu)
        proj_s3 = _in_proj(xs, g_pre_mix[l], w_heads, l, bs * ts, bn_heads).reshape(bs, ts, -1)
        a_s, a_new = _conv_sample(glu_s, 0, state_conv[l].reshape(bs * (CONV_K - 1), conv_w), bs, ts,
                                  w_dw[l], b_dw[l], gn_conv_g[l], gn_conv_b[l])
        b_s, ret_s = _ret_sample(proj_s3, state_ret, l, ret_s, n_ret, 0, cos_s, sin_s, gn_ret_g[l])
        c_s = _sb_sample(proj_s3, cache_kt, cache_vt, l, page_table, n_sb, sq0, sb_bias[l])
        xs = _out_proj(a_s, b_s.reshape(bs * ts, -1), c_s.reshape(bs * ts, -1), w_out_b, l, xs, g_post_mix[l],
                       bs * ts)
        xs, ffn_t0, ffn_t1 = _ffn_sample(xs, g_pre_ffn[l], w_up_b, w_fconv[l], b_fconv[l], w_down_b, l,
                                         g_post_ffn[l], state_ffn[l].reshape(bs, (FFN_K - 1) * d_ff), bs, ts, tf)
        conv_full = jnp.concatenate([state_conv[l], a_new.reshape(bs, ts, conv_w)], axis=1)
        outs[5].append(conv_full[:, -(CONV_K - 1):, :])
        outs[7].append(proj_s3[:, :, sk0:sk0 + sb_w].reshape(bs, ts, n_sb, HEAD_DIM))
        outs[8].append(proj_s3[:, :, sv0:sv0 + sb_w].reshape(bs, ts, n_sb, HEAD_DIM))
        outs[9].append(jnp.stack([ffn_t0, ffn_t1], axis=1))

    stacked = [jnp.stack(o) if o else None for o in outs]
    stacked[2], stacked[3] = (jnp.transpose(p, (0, 1, 2, 4, 3, 5)) for p in pages_kv)
    stacked[6] = ret_s
    return (xp.reshape(bp, tp, d), xs.reshape(bs, ts, d), *stacked)
```

```python
import functools

import numpy as np
import jax
import jax.numpy as jnp
from jax import lax
from jax.experimental import pallas as pl
from jax.experimental.pallas import tpu as pltpu

F32 = jnp.float32
BF16 = jnp.bfloat16

EPS = 1e-6
HEAD_DIM = 128
CONV_K = 31
CONV_PAD = 32
FFN_K = 3
ROPE_BASE = 10000.0
RET_CHUNK = 128
V7X_VMEM_BYTES = 64 * 1024 * 1024
VMEM_LIMIT = V7X_VMEM_BYTES - 8 * 1024 * 1024


def _cparams(sem):
    return pltpu.CompilerParams(dimension_semantics=sem, vmem_limit_bytes=VMEM_LIMIT)


def _rms(x, g):
    ms = jnp.mean(x * x, axis=-1, keepdims=True)
    return x * lax.rsqrt(ms + EPS) * g


def _group_norm(y):
    mu = jnp.mean(y, axis=-1, keepdims=True)
    d = y - mu
    var = jnp.mean(d * d, axis=-1, keepdims=True)
    return d * lax.rsqrt(var + EPS)


def _silu(x):
    return x * jax.nn.sigmoid(x)


def _softplus(z):
    return jnp.maximum(z, 0.0) + jnp.log(1.0 + jnp.exp(-jnp.abs(z)))


def _dot(a, b):
    return jnp.dot(a, b, preferred_element_type=F32)


def _dot_nt(a, b):
    return lax.dot_general(a, b, (((1,), (1,)), ((), ())), preferred_element_type=F32)


def _dot_tn(a, b):
    return lax.dot_general(a, b, (((0,), (0,)), ((), ())), preferred_element_type=F32)


def _in_proj_kernel(x_ref, g_ref, w_ref, o_ref, h_ref):
    @pl.when(pl.program_id(1) == 0)
    def _():
        h_ref[...] = _rms(x_ref[...], g_ref[...]).astype(BF16)

    o_ref[...] = _dot(h_ref[...], w_ref[...])


def _in_proj(x, g, w, layer, bm, bn):
    rows, d = x.shape
    n = w.shape[2]
    return pl.pallas_call(
        _in_proj_kernel,
        grid=(rows // bm, n // bn),
        in_specs=[
            pl.BlockSpec((bm, d), lambda i, j: (i, 0)),
            pl.BlockSpec((1, d), lambda i, j: (0, 0)),
            pl.BlockSpec((None, d, bn), lambda i, j: (layer, 0, j)),
        ],
        out_specs=pl.BlockSpec((bm, bn), lambda i, j: (i, j)),
        out_shape=jax.ShapeDtypeStruct((rows, n), F32),
        scratch_shapes=[pltpu.VMEM((bm, d), BF16)],
        compiler_params=_cparams(("parallel", "arbitrary")),
        name="in_proj",
    )(x, g.reshape(1, d), w)


def _conv_prompt_kernel(val_ref, gate_ref, w_ref, b_ref, gg_ref, gb_ref, o_ref, st_ref, abuf):
    tb = val_ref.shape[0]
    sub = min(tb, 128)

    @pl.when(pl.program_id(2) == 0)
    def _():
        abuf[0:CONV_PAD, :] = jnp.zeros((CONV_PAD, HEAD_DIM), F32)

    abuf[CONV_PAD:CONV_PAD + tb, :] = val_ref[...] * jax.nn.sigmoid(gate_ref[...])
    lead = CONV_PAD - (CONV_K - 1)
    for r0 in range(0, tb, sub):
        y = jnp.broadcast_to(b_ref[...], (sub, HEAD_DIM))
        for k in range(CONV_K):
            y = y + w_ref[k:k + 1, :] * abuf[r0 + k + lead:r0 + k + lead + sub, :]
        yn = _group_norm(y) * gg_ref[...] + gb_ref[...]
        o_ref[r0:r0 + sub, :] = _silu(yn).astype(o_ref.dtype)
    tail = abuf[tb:tb + CONV_PAD, :]
    st_ref[0] = tail
    abuf[0:CONV_PAD, :] = tail


def _conv_prompt(proj, col0, n_seq, seq, w_dw, b_dw, gn_g, gn_b, tb):
    conv_w = w_dw.shape[1]
    groups = conv_w // HEAD_DIM
    nt = seq // tb
    return pl.pallas_call(
        _conv_prompt_kernel,
        grid=(n_seq, groups, nt),
        in_specs=[
            pl.BlockSpec((tb, HEAD_DIM), lambda n, g, t: (n * nt + t, col0 + g)),
            pl.BlockSpec((tb, HEAD_DIM), lambda n, g, t: (n * nt + t, col0 + groups + g)),
            pl.BlockSpec((CONV_K, HEAD_DIM), lambda n, g, t: (0, g)),
            pl.BlockSpec((1, HEAD_DIM), lambda n, g, t: (0, g)),
            pl.BlockSpec((1, HEAD_DIM), lambda n, g, t: (0, g)),
            pl.BlockSpec((1, HEAD_DIM), lambda n, g, t: (0, g)),
        ],
        out_specs=[
            pl.BlockSpec((tb, HEAD_DIM), lambda n, g, t: (n * nt + t, g)),
            pl.BlockSpec((1, CONV_PAD, HEAD_DIM), lambda n, g, t: (n, 0, g)),
        ],
        out_shape=[
            jax.ShapeDtypeStruct((n_seq * seq, conv_w), BF16),
            jax.ShapeDtypeStruct((n_seq, CONV_PAD, conv_w), F32),
        ],
        scratch_shapes=[pltpu.VMEM((CONV_PAD + tb, HEAD_DIM), F32)],
        compiler_params=_cparams(("parallel", "parallel", "arbitrary")),
        name="conv_prompt",
    )(proj, proj, w_dw, b_dw.reshape(1, -1), gn_g.reshape(1, -1), gn_b.reshape(1, -1))


def _conv_sample_kernel(val_ref, gate_ref, st_ref, w_ref, b_ref, gg_ref, gb_ref, o_ref, a_ref, *, nb, ts, n_prev):
    a_ref[...] = val_ref[...] * jax.nn.sigmoid(gate_ref[...])

    def a_full(j):
        if j < n_prev:
            return st_ref[pl.ds(j, nb, stride=n_prev), :]
        return a_ref[pl.ds(j - n_prev, nb, stride=ts), :]

    for t in range(ts):
        y = jnp.broadcast_to(b_ref[...], (nb, HEAD_DIM))
        for k in range(CONV_K):
            y = y + w_ref[k:k + 1, :] * a_full(t + k)
        yn = _group_norm(y) * gg_ref[...] + gb_ref[...]
        o_ref[pl.ds(t, nb, stride=ts), :] = _silu(yn)


def _conv_sample(proj, col0, state2d, nb, ts, w_dw, b_dw, gn_g, gn_b):
    conv_w = w_dw.shape[1]
    groups = conv_w // HEAD_DIM
    n_prev = CONV_K - 1
    rows = nb * ts
    kern = functools.partial(_conv_sample_kernel, nb=nb, ts=ts, n_prev=n_prev)
    return pl.pallas_call(
        kern,
        grid=(groups,),
        in_specs=[
            pl.BlockSpec((rows, HEAD_DIM), lambda g: (0, col0 + g)),
            pl.BlockSpec((rows, HEAD_DIM), lambda g: (0, col0 + groups + g)),
            pl.BlockSpec((nb * n_prev, HEAD_DIM), lambda g: (0, g)),
            pl.BlockSpec((CONV_K, HEAD_DIM), lambda g: (0, g)),
            pl.BlockSpec((1, HEAD_DIM), lambda g: (0, g)),
            pl.BlockSpec((1, HEAD_DIM), lambda g: (0, g)),
            pl.BlockSpec((1, HEAD_DIM), lambda g: (0, g)),
        ],
        out_specs=[
            pl.BlockSpec((rows, HEAD_DIM), lambda g: (0, g)),
            pl.BlockSpec((rows, HEAD_DIM), lambda g: (0, g)),
        ],
        out_shape=[
            jax.ShapeDtypeStruct((rows, conv_w), F32),
            jax.ShapeDtypeStruct((rows, conv_w), F32),
        ],
        compiler_params=_cparams(("parallel",)),
        name="conv_sample",
    )(proj, proj, state2d, w_dw, b_dw.reshape(1, -1), gn_g.reshape(1, -1), gn_b.reshape(1, -1))


def _rope(x, cos2, sin_signed):
    return x * cos2 + pltpu.roll(x, HEAD_DIM // 2, axis=1) * sin_signed


def _ret_prompt_kernel(q_ref, k_ref, v_ref, g_ref, cos_ref, sin_ref, dec_ref, xi_ref, zeta_ref, gl_ref, gn_ref,
                       o_ref, so_ref, s_ref):
    @pl.when(pl.program_id(1) == 0)
    def _():
        s_ref[...] = jnp.zeros_like(s_ref)

    cos2, sin_s = cos_ref[...], sin_ref[...]
    heads = range(s_ref.shape[0])
    lanes = [slice(h * HEAD_DIM, (h + 1) * HEAD_DIM) for h in heads]
    qs = [_rope(q_ref[:, ln], cos2, sin_s).astype(BF16) for ln in lanes]
    ks = [_rope(k_ref[:, ln], cos2, sin_s) * (HEAD_DIM ** -0.5) for ln in lanes]
    vs = [v_ref[:, ln].astype(BF16) for ln in lanes]
    scores = [_dot_nt(qs[h], ks[h].astype(BF16)) * dec_ref[h] for h in heads]
    cross = [_dot(qs[h], s_ref[h].astype(BF16)) * xi_ref[h] for h in heads]
    kv = [_dot_tn((ks[h] * zeta_ref[h]).astype(BF16), vs[h]) for h in heads]
    inner = [_dot(scores[h].astype(BF16), vs[h]) for h in heads]
    for h in heads:
        s_new = gl_ref[h] * s_ref[h] + kv[h]
        s_ref[h] = s_new
        so_ref[0, h] = s_new
        r = inner[h] + cross[h]
        o_ref[:, lanes[h]] = (_group_norm(r) * gn_ref[:, lanes[h]] * _silu(g_ref[:, lanes[h]])).astype(o_ref.dtype)


def _ret_consts(length, n_heads):
    log_gamma = np.log1p(-np.exp2(-5.0 - np.arange(n_heads, dtype=np.float64)))
    idx = np.arange(RET_CHUNK, dtype=np.float64)
    diff = idx[:, None] - idx[None, :]
    live = (diff >= 0) & (idx[:, None] < length) & (idx[None, :] < length)
    dec = np.where(live[None], np.exp(np.maximum(diff, 0.0)[None] * log_gamma[:, None, None]), 0.0)
    xi = np.exp((idx[None, :] + 1.0) * log_gamma[:, None])
    zeta = np.where(idx[None, :] < length, np.exp((length - 1.0 - idx)[None, :] * log_gamma[:, None]), 0.0)
    gl = np.exp(length * log_gamma)
    bc = lambda a: jnp.asarray(np.broadcast_to(a[:, :, None], (n_heads, RET_CHUNK, HEAD_DIM)), F32)
    return (jnp.asarray(dec, F32), bc(xi), bc(zeta),
            jnp.asarray(np.broadcast_to(gl[:, None, None], (n_heads, 1, HEAD_DIM)), F32))


def _rope_tables(pos):
    half = HEAD_DIM // 2
    inv = ROPE_BASE ** (-jnp.arange(half, dtype=F32) / half)
    ang = pos.astype(F32)[:, None] * inv[None, :]
    cos, sin = jnp.cos(ang), jnp.sin(ang)
    return jnp.concatenate([cos, cos], axis=1), jnp.concatenate([-sin, sin], axis=1)


def _ret_prompt(proj, n_seq, seq, n_heads, seg0, cos2, sin_s, gn_g):
    nc = seq // RET_CHUNK
    width = n_heads * HEAD_DIM
    dec, xi, zeta, gl = _ret_consts(RET_CHUNK, n_heads)
    blk = lambda off: pl.BlockSpec((RET_CHUNK, width), lambda n, c: (n * nc + c, seg0 + off))
    full = lambda a: pl.BlockSpec(a.shape, lambda n, c: (0,) * a.ndim)
    return pl.pallas_call(
        _ret_prompt_kernel,
        grid=(n_seq, nc),
        in_specs=[
            blk(0), blk(1), blk(2), blk(3),
            pl.BlockSpec((RET_CHUNK, HEAD_DIM), lambda n, c: (c, 0)),
            pl.BlockSpec((RET_CHUNK, HEAD_DIM), lambda n, c: (c, 0)),
            full(dec), full(xi), full(zeta), full(gl),
            pl.BlockSpec((1, width), lambda n, c: (0, 0)),
        ],
        out_specs=[
            pl.BlockSpec((RET_CHUNK, width), lambda n, c: (n * nc + c, 0)),
            pl.BlockSpec((1, n_heads, HEAD_DIM, HEAD_DIM), lambda n, c: (n, 0, 0, 0)),
        ],
        out_shape=[
            jax.ShapeDtypeStruct((n_seq * seq, width), BF16),
            jax.ShapeDtypeStruct((n_seq, n_heads, HEAD_DIM, HEAD_DIM), F32),
        ],
        scratch_shapes=[pltpu.VMEM((n_heads, HEAD_DIM, HEAD_DIM), F32)],
        compiler_params=_cparams(("parallel", "arbitrary")),
        name="ret_prompt",
    )(proj, proj, proj, proj, cos2, sin_s, dec, xi, zeta, gl, gn_g.reshape(1, -1))


def _ret_sample_kernel(p_ref, s_ref, cos_ref, sin_ref, dec_ref, xi_ref, zeta_ref, gl_ref, gn_ref, *refs,
                       n_heads, col0, ts):
    o_ref, so_ref, qp, kp, kzp, vp = refs[-6:]
    cos2, sin_s = cos_ref[...], sin_ref[...]
    heads = range(n_heads)
    cols = lambda off, h: slice((col0 + off * n_heads + h) * HEAD_DIM, (col0 + off * n_heads + h + 1) * HEAD_DIM)
    qp[...] = jnp.zeros_like(qp)
    kp[...] = jnp.zeros_like(kp)
    kzp[...] = jnp.zeros_like(kzp)
    vp[...] = jnp.zeros_like(vp)
    for h in heads:
        k = _rope(p_ref[0, :, cols(1, h)], cos2, sin_s) * (HEAD_DIM ** -0.5)
        qp[h, 0:ts, :] = _rope(p_ref[0, :, cols(0, h)], cos2, sin_s)
        kp[h, 0:ts, :] = k
        kzp[h, 0:ts, :] = k * zeta_ref[h, 0:ts, :]
        vp[h, 0:ts, :] = p_ref[0, :, cols(2, h)]
    qs = [qp[h].astype(BF16) for h in heads]
    vs = [vp[h].astype(BF16) for h in heads]
    scores = [_dot_nt(qs[h], kp[h].astype(BF16)) * dec_ref[h, 0:8, :] for h in heads]
    cross = [_dot(qs[h], s_ref[0, h].astype(BF16)) * xi_ref[h, 0:8, :] for h in heads]
    kv = [_dot_tn(kzp[h].astype(BF16), vs[h]) for h in heads]
    inner = [_dot(scores[h].astype(BF16), vs[h]) for h in heads]
    for h in heads:
        so_ref[0, h] = gl_ref[h] * s_ref[0, h] + kv[h]
        r = (inner[h] + cross[h])[0:ts]
        gn = gn_ref[:, h * HEAD_DIM:(h + 1) * HEAD_DIM]
        o_ref[0, :, h * HEAD_DIM:(h + 1) * HEAD_DIM] = _group_norm(r) * gn * _silu(p_ref[0, :, cols(3, h)])


def _ret_sample(proj3, state, layer, new_state, n_heads, col0, cos2, sin_s, gn_g):
    nb, ts, n_cols = proj3.shape
    dec, xi, zeta, gl = _ret_consts(ts, n_heads)
    kern = functools.partial(_ret_sample_kernel, n_heads=n_heads, col0=col0, ts=ts)
    full = lambda a: pl.BlockSpec(a.shape, lambda b: (0,) * a.ndim)
    state_spec = pl.BlockSpec((None, 1, n_heads, HEAD_DIM, HEAD_DIM), lambda b: (layer, b, 0, 0, 0))
    earlier = [] if new_state is None else [new_state]
    return pl.pallas_call(
        kern,
        grid=(nb,),
        in_specs=[
            pl.BlockSpec((1, ts, n_cols), lambda b: (b, 0, 0)),
            state_spec,
            full(cos2), full(sin_s), full(dec), full(xi), full(zeta), full(gl),
            pl.BlockSpec((1, n_heads * HEAD_DIM), lambda b: (0, 0)),
        ] + [pl.BlockSpec(memory_space=pl.ANY)] * len(earlier),
        out_specs=[pl.BlockSpec((1, ts, n_heads * HEAD_DIM), lambda b: (b, 0, 0)), state_spec],
        out_shape=[
            jax.ShapeDtypeStruct((nb, ts, n_heads * HEAD_DIM), F32),
            jax.ShapeDtypeStruct(state.shape, F32),
        ],
        input_output_aliases={9 + a: 1 + a for a in range(len(earlier))},
        scratch_shapes=[
            pltpu.VMEM((n_heads, 8, HEAD_DIM), F32),
            pltpu.VMEM((n_heads, RET_CHUNK, HEAD_DIM), F32),
            pltpu.VMEM((n_heads, RET_CHUNK, HEAD_DIM), F32),
            pltpu.VMEM((n_heads, RET_CHUNK, HEAD_DIM), F32),
        ],
        compiler_params=_cparams(("parallel",)),
        name="ret_sample",
    )(proj3, state, cos2, sin_s, dec, xi, zeta, gl, gn_g.reshape(1, -1), *earlier)


SB_BLOCK = 128
SB_SWEEP = 3


def _suffix_matrix():
    j = np.arange(SB_BLOCK)
    strict = -(j[:, None] > j[None, :]).astype(np.float32)
    half = np.concatenate([strict, -np.ones((SB_BLOCK, SB_BLOCK), np.float32)], axis=1)
    return jnp.asarray(np.concatenate([half, half], axis=0), BF16)


def _suffix_sums(sp, valid, suffix):
    drop = sp if valid is None else jnp.where(valid, sp, 0.0)
    hi = drop.astype(BF16)
    lo = (drop - hi.astype(F32)).astype(BF16)
    return _dot(jnp.concatenate([hi, lo], axis=1), suffix)


def _sb_prompt_kernel(q_ref, k_ref, v_ref, bias_ref, suf_ref, *refs, n_heads):
    o_ref, pk_ref, pv_ref, qb_ref, kb_ref, vb_ref, acc_ref, c_ref = refs[-8:]
    i = pl.program_id(1)

    @pl.when(i == 0)
    def _():
        kb_ref[...] = k_ref[...].astype(BF16)
        vb_ref[...] = v_ref[...].astype(BF16)

    qb_ref[...] = q_ref[...].astype(BF16)
    acc_ref[...] = jnp.zeros_like(acc_ref)
    c_ref[...] = jnp.zeros_like(c_ref)

    lanes = [slice(h * HEAD_DIM, (h + 1) * HEAD_DIM) for h in range(n_heads)]
    own = pl.ds(pl.multiple_of(i * SB_BLOCK, SB_BLOCK), SB_BLOCK)
    for h, ln in enumerate(lanes):
        pk_ref[h] = k_ref[own, ln]
        pv_ref[h] = v_ref[own, ln]

    def sweep(blocks, valid):
        offs = [j * SB_BLOCK if isinstance(j, int) else pl.multiple_of(j * SB_BLOCK, SB_BLOCK) for j in blocks]
        zs = [[_dot_nt(qb_ref[:, ln], kb_ref[pl.ds(off, SB_BLOCK), ln]) * (HEAD_DIM ** -0.5) + bias_ref[:, ln]
               for ln in lanes] for off in offs]
        sps = [[_softplus(z) for z in row_z] for row_z in zs]
        cums = [[_suffix_sums(sp, valid, suf_ref[...]) for sp in row_sp] for row_sp in sps]
        for h, ln in enumerate(lanes):
            c = c_ref[:, ln]
            pv = None
            for b, off in enumerate(offs):
                w = jnp.exp(zs[b][h] - sps[b][h] + cums[b][h][:, :SB_BLOCK] + c)
                if valid is not None:
                    w = jnp.where(valid, w, 0.0)
                d = _dot(w.astype(BF16), vb_ref[pl.ds(off, SB_BLOCK), ln])
                pv = d if pv is None else pv + d
                c = c + cums[b][h][:, SB_BLOCK:]
            acc_ref[:, ln] += pv
            c_ref[:, ln] = c

    row = lax.broadcasted_iota(jnp.int32, (SB_BLOCK, SB_BLOCK), 0)
    col = lax.broadcasted_iota(jnp.int32, (SB_BLOCK, SB_BLOCK), 1)
    sweep([i], col < row)

    def body(t, carry):
        j = i - 1 - SB_SWEEP * t
        sweep([j - s for s in range(SB_SWEEP)], None)
        return carry

    trips = lax.div(i, jnp.int32(SB_SWEEP))
    lax.fori_loop(0, trips, body, 0)
    rem = i - trips * SB_SWEEP
    for r in range(1, SB_SWEEP):
        @pl.when(rem == r)
        def _(r=r):
            sweep(list(range(r - 1, -1, -1)), None)

    o_ref[...] = acc_ref[...].astype(o_ref.dtype)


def _sb_prompt(proj, n_seq, seq, n_heads, seg0, sb_bias, layer, depth, pages_kv):
    nq = seq // SB_BLOCK
    width = n_heads * HEAD_DIM
    suffix = _suffix_matrix()
    bias_row = jnp.repeat(sb_bias.astype(F32), HEAD_DIM).reshape(1, width)
    kern = functools.partial(_sb_prompt_kernel, n_heads=n_heads)
    pages_shape = jax.ShapeDtypeStruct((depth, n_seq, nq, n_heads, SB_BLOCK, HEAD_DIM), F32)
    pages_spec = pl.BlockSpec((None, None, None, n_heads, SB_BLOCK, HEAD_DIM), lambda n, i: (layer, n, i, 0, 0, 0))
    earlier = [] if pages_kv is None else list(pages_kv)
    return pl.pallas_call(
        kern,
        grid=(n_seq, nq),
        in_specs=[
            pl.BlockSpec((SB_BLOCK, width), lambda n, i: (n * nq + i, seg0)),
            pl.BlockSpec((seq, width), lambda n, i: (n, seg0 + 1)),
            pl.BlockSpec((seq, width), lambda n, i: (n, seg0 + 2)),
            pl.BlockSpec((1, width), lambda n, i: (0, 0)),
            pl.BlockSpec(suffix.shape, lambda n, i: (0, 0)),
        ] + [pl.BlockSpec(memory_space=pl.ANY)] * len(earlier),
        out_specs=[pl.BlockSpec((SB_BLOCK, width), lambda n, i: (n * nq + i, 0)), pages_spec, pages_spec],
        out_shape=[jax.ShapeDtypeStruct((n_seq * seq, width), BF16), pages_shape, pages_shape],
        input_output_aliases={5 + a: 1 + a for a in range(len(earlier))},
        scratch_shapes=[
            pltpu.VMEM((SB_BLOCK, width), BF16),
            pltpu.VMEM((seq, width), BF16),
            pltpu.VMEM((seq, width), BF16),
            pltpu.VMEM((SB_BLOCK, width), F32),
            pltpu.VMEM((SB_BLOCK, width), F32),
        ],
        compiler_params=_cparams(("parallel", "arbitrary")),
        name="sb_prompt",
    )(proj, proj, proj, bias_row, suffix, *earlier)


Q_TILE = 8


def _sb_sample_kernel(pt_ref, p_ref, *refs, n_heads, ts, n_pages, sq0):
    k_refs, v_refs = refs[:n_pages], refs[n_pages:2 * n_pages]
    bias_ref, suf_ref, o_ref, q8, knew, vnew, z_scr, w_scr = refs[2 * n_pages:]
    width = n_heads * HEAD_DIM
    grp = n_heads * Q_TILE

    q8[...] = jnp.zeros_like(q8)
    knew[...] = jnp.zeros_like(knew)
    vnew[...] = jnp.zeros_like(vnew)
    for h in range(n_heads):
        c0 = sq0 + h * HEAD_DIM
        q8[h, 0:ts, :] = p_ref[0, :, c0:c0 + HEAD_DIM]
        knew[h, 0:ts, :] = p_ref[0, :, c0 + width:c0 + width + HEAD_DIM]
        vnew[h, 0:ts, :] = p_ref[0, :, c0 + 2 * width:c0 + 2 * width + HEAD_DIM]

    def tile_of(page_refs, own, p, h):
        return (page_refs[p][h] if p < n_pages else own[h]).astype(BF16)

    for p in range(n_pages + 1):
        for h in range(n_heads):
            r0 = (p * n_heads + h) * Q_TILE
            z = _dot_nt(q8[h].astype(BF16), tile_of(k_refs, knew, p, h))
            z_scr[r0:r0 + Q_TILE, :] = z * (HEAD_DIM ** -0.5) + bias_ref[h]

    z = z_scr[...]
    row = lax.broadcasted_iota(jnp.int32, z.shape, 0)
    col = lax.broadcasted_iota(jnp.int32, z.shape, 1)
    valid = (row < n_pages * grp) | (col < jnp.bitwise_and(row, Q_TILE - 1))
    sp = _softplus(z)
    cum = _suffix_sums(sp, valid, suf_ref[...])
    tot = cum[:, SB_BLOCK:]
    later = [None] * (n_pages + 1)
    run = jnp.zeros((grp, SB_BLOCK), F32)
    for p in reversed(range(n_pages + 1)):
        later[p] = run
        run = run + tot[p * grp:(p + 1) * grp]
    c = jnp.concatenate(later, axis=0)
    w_scr[...] = jnp.where(valid, jnp.exp(z - sp + cum[:, :SB_BLOCK] + c), 0.0)

    for h in range(n_heads):
        acc = jnp.zeros((Q_TILE, HEAD_DIM), F32)
        for p in range(n_pages + 1):
            r0 = (p * n_heads + h) * Q_TILE
            acc = acc + _dot(w_scr[r0:r0 + Q_TILE, :].astype(BF16), tile_of(v_refs, vnew, p, h))
        o_ref[0, :, h * HEAD_DIM:(h + 1) * HEAD_DIM] = acc[0:ts]


def _sb_sample(proj3, cache_k, cache_v, layer, page_table, n_heads, sq0, sb_bias):
    nb, ts, n_cols = proj3.shape
    n_pages = page_table.shape[1]
    page = cache_k.shape[3]
    assert page == SB_BLOCK and ts < Q_TILE
    width = n_heads * HEAD_DIM
    rows = (n_pages + 1) * n_heads * Q_TILE
    suffix = _suffix_matrix()
    bias = jnp.broadcast_to(sb_bias.astype(F32)[:, None, None], (n_heads, 1, HEAD_DIM))
    kern = functools.partial(_sb_sample_kernel, n_heads=n_heads, ts=ts, n_pages=n_pages, sq0=sq0)
    page_spec = lambda p: pl.BlockSpec((None, None, n_heads, page, HEAD_DIM),
                                       lambda b, pt: (layer, pt[b * n_pages + p], 0, 0, 0))
    grid_spec = pltpu.PrefetchScalarGridSpec(
        num_scalar_prefetch=1,
        grid=(nb,),
        in_specs=(
            [pl.BlockSpec((1, ts, n_cols), lambda b, pt: (b, 0, 0))]
            + [page_spec(p) for p in range(n_pages)] * 2
            + [pl.BlockSpec(bias.shape, lambda b, pt: (0, 0, 0)),
               pl.BlockSpec(suffix.shape, lambda b, pt: (0, 0))]
        ),
        out_specs=pl.BlockSpec((1, ts, width), lambda b, pt: (b, 0, 0)),
        scratch_shapes=[
            pltpu.VMEM((n_heads, Q_TILE, HEAD_DIM), F32),
            pltpu.VMEM((n_heads, page, HEAD_DIM), F32),
            pltpu.VMEM((n_heads, page, HEAD_DIM), F32),
            pltpu.VMEM((rows, SB_BLOCK), F32),
            pltpu.VMEM((rows, SB_BLOCK), F32),
        ],
    )
    return pl.pallas_call(
        kern,
        grid_spec=grid_spec,
        out_shape=jax.ShapeDtypeStruct((nb, ts, width), F32),
        compiler_params=_cparams(("parallel",)),
        name="sb_sample",
    )(page_table.reshape(-1), proj3, *([cache_k] * n_pages), *([cache_v] * n_pages), bias, suffix)


def _out_proj_kernel(a_ref, b_ref, c_ref, w_ref, x_ref, g_ref, o_ref, *, wa, wb):
    mix = (_dot(a_ref[...].astype(BF16), w_ref[0:wa, :])
           + _dot(b_ref[...].astype(BF16), w_ref[wa:wa + wb, :])
           + _dot(c_ref[...].astype(BF16), w_ref[wa + wb:, :]))
    o_ref[...] = x_ref[...] + _rms(mix, g_ref[...])


def _out_proj(a, b, c, w, layer, x, g, bm):
    rows, d = x.shape
    wa, wb, wc = a.shape[1], b.shape[1], c.shape[1]
    kern = functools.partial(_out_proj_kernel, wa=wa, wb=wb)
    return pl.pallas_call(
        kern,
        grid=(rows // bm,),
        in_specs=[
            pl.BlockSpec((bm, wa), lambda i: (i, 0)),
            pl.BlockSpec((bm, wb), lambda i: (i, 0)),
            pl.BlockSpec((bm, wc), lambda i: (i, 0)),
            pl.BlockSpec((None,) + w.shape[1:], lambda i: (layer, 0, 0)),
            pl.BlockSpec((bm, d), lambda i: (i, 0)),
            pl.BlockSpec((1, d), lambda i: (0, 0)),
        ],
        out_specs=pl.BlockSpec((bm, d), lambda i: (i, 0)),
        out_shape=jax.ShapeDtypeStruct((rows, d), F32),
        compiler_params=_cparams(("parallel",)),
        name="out_proj",
    )(a, b, c, w, x, g.reshape(1, d))


def _ffn_prompt_kernel(x_ref, gpre_ref, wu_ref, wg_ref, wc_ref, bc_ref, wd_ref, gpost_ref,
                       o_ref, tail_ref, h_ref, acc_ref, carry_ref, *, blocks_per_seq):
    i, j = pl.program_id(0), pl.program_id(1)
    bm = x_ref.shape[0]

    @pl.when(j == 0)
    def _():
        h_ref[...] = _rms(x_ref[...], gpre_ref[...]).astype(BF16)
        acc_ref[...] = jnp.zeros_like(acc_ref)

    @pl.when(lax.rem(i, blocks_per_seq) == 0)
    def _():
        carry_ref[j] = jnp.zeros(carry_ref.shape[1:], F32)

    h = h_ref[...]
    g = _dot(h, wg_ref[...])
    prev = carry_ref[j]
    row = lax.broadcasted_iota(jnp.int32, g.shape, 0)
    g1 = jnp.where(row == 0, prev[7:8, :], pltpu.roll(g, 1, axis=0))
    g2 = jnp.where(row == 0, prev[6:7, :], jnp.where(row == 1, prev[7:8, :], pltpu.roll(g, 2, axis=0)))
    gate = _silu(wc_ref[0:1, :] * g2 + wc_ref[1:2, :] * g1 + wc_ref[2:3, :] * g + bc_ref[...])
    u = _dot(h, wu_ref[...])
    acc_ref[...] += _dot((gate * u).astype(BF16), wd_ref[...])
    tail = g[bm - 8:bm, :]
    carry_ref[j] = tail
    tail_ref[0] = tail

    @pl.when(j == pl.num_programs(1) - 1)
    def _():
        o_ref[...] = x_ref[...] + _rms(acc_ref[...], gpost_ref[...])


def _ffn_sample_kernel(x_ref, gpre_ref, wu_ref, wg_ref, wc_ref, bc_ref, wd_ref, gpost_ref, prev0_ref, prev1_ref,
                       o_ref, tail0_ref, tail1_ref, h_ref, acc_ref, g_scr, u_scr, act_scr, *, nb, ts):
    j = pl.program_id(0)
    n_prev = FFN_K - 1
    prev_refs = (prev0_ref, prev1_ref)
    tail_refs = (tail0_ref, tail1_ref)
    n_chunks = g_scr.shape[0]

    @pl.when(j == 0)
    def _():
        h_ref[...] = _rms(x_ref[...], gpre_ref[...]).astype(BF16)
        acc_ref[...] = jnp.zeros_like(acc_ref)

    h = h_ref[...]
    u = _dot(h, wu_ref[...])
    g = _dot(h, wg_ref[...])
    for c in range(n_chunks):
        lanes = slice(c * HEAD_DIM, (c + 1) * HEAD_DIM)
        u_scr[c] = u[:, lanes]
        g_scr[c] = g[:, lanes]

    for c in range(n_chunks):
        lanes = slice(c * HEAD_DIM, (c + 1) * HEAD_DIM)

        def g_full(t):
            if t < n_prev:
                return prev_refs[t][:, lanes]
            return g_scr[c, pl.ds(t - n_prev, nb, stride=ts), :]

        for t in range(ts):
            gc = bc_ref[:, lanes]
            for k in range(FFN_K):
                gc = gc + wc_ref[k:k + 1, lanes] * g_full(t + k)
            act_scr[c, pl.ds(t, nb, stride=ts), :] = _silu(gc) * u_scr[c, pl.ds(t, nb, stride=ts), :]
        for t in range(n_prev):
            tail_refs[t][:, lanes] = g_full(ts + t)
        acc_ref[...] += _dot(act_scr[c].astype(BF16), wd_ref[lanes, :])

    @pl.when(j == pl.num_programs(0) - 1)
    def _():
        o_ref[...] = x_ref[...] + _rms(acc_ref[...], gpost_ref[...])


def _ffn_prompt(x, gpre, w_up, w_fconv, b_fconv, w_down, layer, gpost, seq, bm, tf):
    rows, d = x.shape
    d_ff = w_down.shape[1]
    nf = d_ff // tf
    kern = functools.partial(_ffn_prompt_kernel, blocks_per_seq=seq // bm)
    return pl.pallas_call(
        kern,
        grid=(rows // bm, nf),
        in_specs=[
            pl.BlockSpec((bm, d), lambda i, j: (i, 0)),
            pl.BlockSpec((1, d), lambda i, j: (0, 0)),
            pl.BlockSpec((None, d, tf), lambda i, j: (layer, 0, j)),
            pl.BlockSpec((None, d, tf), lambda i, j: (layer, 0, nf + j)),
            pl.BlockSpec((FFN_K, tf), lambda i, j: (0, j)),
            pl.BlockSpec((1, tf), lambda i, j: (0, j)),
            pl.BlockSpec((None, tf, d), lambda i, j: (layer, j, 0)),
            pl.BlockSpec((1, d), lambda i, j: (0, 0)),
        ],
        out_specs=[
            pl.BlockSpec((bm, d), lambda i, j: (i, 0)),
            pl.BlockSpec((1, 8, tf), lambda i, j: (i, 0, j)),
        ],
        out_shape=[
            jax.ShapeDtypeStruct((rows, d), F32),
            jax.ShapeDtypeStruct((rows // bm, 8, d_ff), F32),
        ],
        scratch_shapes=[
            pltpu.VMEM((bm, d), BF16),
            pltpu.VMEM((bm, d), F32),
            pltpu.VMEM((nf, 8, tf), F32),
        ],
        compiler_params=_cparams(("arbitrary", "arbitrary")),
        name="ffn_prompt",
    )(x, gpre.reshape(1, d), w_up, w_up, w_fconv, b_fconv.reshape(1, -1), w_down, gpost.reshape(1, d))


def _ffn_sample(x, gpre, w_up, w_fconv, b_fconv, w_down, layer, gpost, prev, nb, ts, tf):
    rows, d = x.shape
    d_ff = w_down.shape[1]
    nf = d_ff // tf
    kern = functools.partial(_ffn_sample_kernel, nb=nb, ts=ts)
    chunk = pltpu.VMEM((tf // HEAD_DIM, rows, HEAD_DIM), F32)
    return pl.pallas_call(
        kern,
        grid=(nf,),
        in_specs=[
            pl.BlockSpec((rows, d), lambda j: (0, 0)),
            pl.BlockSpec((1, d), lambda j: (0, 0)),
            pl.BlockSpec((None, d, tf), lambda j: (layer, 0, j)),
            pl.BlockSpec((None, d, tf), lambda j: (layer, 0, nf + j)),
            pl.BlockSpec((FFN_K, tf), lambda j: (0, j)),
            pl.BlockSpec((1, tf), lambda j: (0, j)),
            pl.BlockSpec((None, tf, d), lambda j: (layer, j, 0)),
            pl.BlockSpec((1, d), lambda j: (0, 0)),
            pl.BlockSpec((nb, tf), lambda j: (0, j)),
            pl.BlockSpec((nb, tf), lambda j: (0, nf + j)),
        ],
        out_specs=[
            pl.BlockSpec((rows, d), lambda j: (0, 0)),
            pl.BlockSpec((nb, tf), lambda j: (0, j)),
            pl.BlockSpec((nb, tf), lambda j: (0, j)),
        ],
        out_shape=[
            jax.ShapeDtypeStruct((rows, d), F32),
            jax.ShapeDtypeStruct((nb, d_ff), F32),
            jax.ShapeDtypeStruct((nb, d_ff), F32),
        ],
        scratch_shapes=[pltpu.VMEM((rows, d), BF16), pltpu.VMEM((rows, d), F32), chunk, chunk, chunk],
        compiler_params=_cparams(("arbitrary",)),
        name="ffn_sample",
    )(x, gpre.reshape(1, d), w_up, w_up, w_fconv, b_fconv.reshape(1, -1), w_down, gpost.reshape(1, d), prev, prev)


def _pick(total, want):
    if total <= want:
        return total
    best = None
    for cand in range(128, want + 1, 128):
        if total % cand == 0:
            best = cand
    assert best is not None, (total, want)
    return best


def kernel(x_prompt, x_sample, cache_k, cache_v, state_conv, state_ret, state_ffn, page_table, g_pre_mix, w_in, w_dw, b_dw, gn_conv_g, gn_conv_b, gn_ret_g, sb_bias, w_out, g_post_mix, g_pre_ffn, w_up, w_fconv, b_fconv, w_down, g_post_ffn):
    bp, tp, d = x_prompt.shape
    bs, ts, _ = x_sample.shape
    depth = w_in.shape[0]
    conv_w = w_dw.shape[2]
    n_ret = gn_ret_g.shape[1] // HEAD_DIM
    n_sb = sb_bias.shape[1]
    d_ff = w_down.shape[1]
    past_len = page_table.shape[1] * cache_k.shape[2]
    assert conv_w % HEAD_DIM == 0 and tp % RET_CHUNK == 0 and tp % SB_BLOCK == 0 and ts < 8

    ret_w, sb_w = n_ret * HEAD_DIM, n_sb * HEAD_DIM
    glu_w = 2 * conv_w
    sq0 = 4 * ret_w
    sk0, sv0 = sq0 + sb_w, sq0 + 2 * sb_w
    assert sq0 % sb_w == 0
    w_glu = w_in[:, :, :glu_w].astype(BF16)
    w_heads = w_in[:, :, glu_w:].astype(BF16)
    w_out_b = w_out.astype(BF16)
    w_up_b = w_up.astype(BF16)
    w_down_b = w_down.astype(BF16)
    page = cache_k.shape[2]
    assert page == SB_BLOCK
    cache_kt = jnp.transpose(cache_k, (0, 1, 3, 2, 4))
    cache_vt = jnp.transpose(cache_v, (0, 1, 3, 2, 4))

    bm_p = _pick(bp * tp, 1024)
    bm_out = _pick(bp * tp, 512)
    bm_seq = _pick(tp, 512)
    bn_glu = _pick(glu_w, 1792)
    bn_heads = _pick(w_heads.shape[2], 1792)
    tf = _pick(d_ff, 512)

    cos_p, sin_p = _rope_tables(jnp.arange(tp))
    cos_s, sin_s = _rope_tables(past_len + jnp.arange(ts))

    xp = x_prompt.reshape(bp * tp, d)
    xs = x_sample.reshape(bs * ts, d)
    outs = [[] for _ in range(10)]
    pages_kv, ret_s = None, None
    for l in range(depth):
        glu = _in_proj(xp, g_pre_mix[l], w_glu, l, bm_p, bn_glu)
        proj = _in_proj(xp, g_pre_mix[l], w_heads, l, bm_p, bn_heads)
        a_out, conv_tail = _conv_prompt(glu, 0, bp, tp, w_dw[l], b_dw[l], gn_conv_g[l], gn_conv_b[l], bm_seq)
        b_out, ret_state = _ret_prompt(proj, bp, tp, n_ret, 0, cos_p, sin_p, gn_ret_g[l])
        c_out, *pages_kv = _sb_prompt(proj, bp, tp, n_sb, sq0 // sb_w, sb_bias[l], l, depth, pages_kv)
        xp = _out_proj(a_out, b_out, c_out, w_out_b, l, xp, g_post_mix[l], bm_out)
        xp, ffn_tail = _ffn_prompt(xp, g_pre_ffn[l], w_up_b, w_fconv[l], b_fconv[l], w_down_b, l, g_post_ffn[l],
                                   tp, bm_seq, tf)
        outs[0].append(conv_tail[:, CONV_PAD - (CONV_K - 1):, :])
        outs[1].append(ret_state)
        outs[4].append(ffn_tail.reshape(bp, tp // bm_seq, 8, d_ff)[:, -1, 8 - (FFN_K - 1):, :])

        glu_s = _in_proj(xs, g_pre_mix[l], w_glu, l, bs * ts, bn_glu)
        proj_s3 = _in_proj(xs, g_pre_mix[l], w_heads, l, bs * ts, bn_heads).reshape(bs, ts, -1)
        a_s, a_new = _conv_sample(glu_s, 0, state_conv[l].reshape(bs * (CONV_K - 1), conv_w), bs, ts,
                                  w_dw[l], b_dw[l], gn_conv_g[l], gn_conv_b[l])
        b_s, ret_s = _ret_sample(proj_s3, state_ret, l, ret_s, n_ret, 0, cos_s, sin_s, gn_ret_g[l])
        c_s = _sb_sample(proj_s3, cache_kt, cache_vt, l, page_table, n_sb, sq0, sb_bias[l])
        xs = _out_proj(a_s, b_s.reshape(bs * ts, -1), c_s.reshape(bs * ts, -1), w_out_b, l, xs, g_post_mix[l],
                       bs * ts)
        xs, ffn_t0, ffn_t1 = _ffn_sample(xs, g_pre_ffn[l], w_up_b, w_fconv[l], b_fconv[l], w_down_b, l,
                                         g_post_ffn[l], state_ffn[l].reshape(bs, (FFN_K - 1) * d_ff), bs, ts, tf)
        conv_full = jnp.concatenate([state_conv[l], a_new.reshape(bs, ts, conv_w)], axis=1)
        outs[5].append(conv_full[:, -(CONV_K - 1):, :])
        outs[7].append(proj_s3[:, :, sk0:sk0 + sb_w].reshape(bs, ts, n_sb, HEAD_DIM))
        outs[8].append(proj_s3[:, :, sv0:sv0 + sb_w].reshape(bs, ts, n_sb, HEAD_DIM))
        outs[9].append(jnp.stack([ffn_t0, ffn_t1], axis=1))

    stacked = [jnp.stack(o) if o else None for o in outs]
    stacked[2], stacked[3] = (jnp.transpose(p, (0, 1, 2, 4, 3, 5)) for p in pages_kv)
    stacked[6] = ret_s
    return (xp.reshape(bp, tp, d), xs.reshape(bs, ts, d), *stacked)
```
